```python
import math
import jax, jax.numpy as jnp
from jax import lax
import numpy as np

D_MODEL = 1024
BATCH = 8
SEQ = 2048
DEPTH = 1

N_META = 16
BLOCK = 128
WINDOW = 128
PAD = (-N_META) % BLOCK
NORM_EPS = 1e-6

ATT_HEAD_DIM = 64
ATT_Q_HEADS = D_MODEL // ATT_HEAD_DIM
ATT_KV_HEADS = 4
ATT_GROUP = ATT_Q_HEADS // ATT_KV_HEADS
ATT_WIDTH = ATT_Q_HEADS * ATT_HEAD_DIM
KV_WIDTH = ATT_KV_HEADS * ATT_HEAD_DIM

SSM_INNER = 2 * D_MODEL
SSM_HEAD_DIM = 64
SSM_HEADS = SSM_INNER // SSM_HEAD_DIM
SSM_GROUPS = 4
SSM_HEADS_PER_GROUP = SSM_HEADS // SSM_GROUPS
SSM_STATE = 128
CONV_WIDTH = 4
CONV_DIM = SSM_INNER + 2 * SSM_GROUPS * SSM_STATE

SPLIT_SIZES = (ATT_WIDTH, KV_WIDTH, KV_WIDTH, ATT_WIDTH, SSM_INNER, CONV_DIM, SSM_HEADS, D_MODEL, D_MODEL)
SPLIT_POINTS = tuple(int(s) for s in np.cumsum(SPLIT_SIZES)[:-1])
IN_PROJ_DIM = int(sum(SPLIT_SIZES))

kernel_name = 'hybrid_swa_sink_alibi_ssd_gated_merge'


def rmsnorm(x, g):
    xf = x.astype(jnp.float32)
    y = xf * lax.rsqrt(jnp.mean(xf * xf, axis=-1, keepdims=True) + NORM_EPS) * g.astype(jnp.float32)
    return y.astype(x.dtype)


def alibi_slopes():
    return jnp.asarray(np.array([2.0 ** (-8.0 * (h + 1) / ATT_Q_HEADS) for h in range(ATT_Q_HEADS)], np.float32))


def sliding_window_attention(q, k, v, sinks):
    b, lp, _ = q.shape
    nb = lp // BLOCK
    km = k[:, PAD:PAD + N_META].reshape(b, N_META, ATT_KV_HEADS, ATT_HEAD_DIM)
    vm = v[:, PAD:PAD + N_META].reshape(b, N_META, ATT_KV_HEADS, ATT_HEAD_DIM)
    qb = q.reshape(b, nb, BLOCK, ATT_KV_HEADS, ATT_GROUP, ATT_HEAD_DIM) * (ATT_HEAD_DIM ** -0.5)

    def with_prev(t):
        t = t.reshape(b, nb, BLOCK, ATT_KV_HEADS, ATT_HEAD_DIM)
        prev = jnp.concatenate([jnp.zeros_like(t[:, :1]), t[:, :-1]], axis=1)
        return jnp.concatenate([prev, t], axis=2)

    kb, vb = with_prev(k), with_prev(v)

    q_pos = jnp.arange(nb)[:, None] * BLOCK + jnp.arange(BLOCK)[None, :]
    k_pos = jnp.arange(nb)[:, None] * BLOCK - BLOCK + jnp.arange(2 * BLOCK)[None, :]
    rel = q_pos[:, :, None] - k_pos[:, None, :]
    band_ok = (rel >= 0) & (rel < WINDOW) & (k_pos[:, None, :] >= PAD + N_META)
    meta_pos = PAD + jnp.arange(N_META)
    meta_ok = meta_pos[None, None, :] <= q_pos[:, :, None]

    slopes = alibi_slopes().reshape(1, 1, ATT_KV_HEADS, ATT_GROUP, 1, 1)
    s_band = jnp.einsum('bnqkgd,bnskd->bnkgqs', qb, kb).astype(jnp.float32)
    s_band = s_band - slopes * rel[None, :, None, None].astype(jnp.float32)
    s_band = jnp.where(band_ok[None, :, None, None], s_band, -jnp.inf)
    s_meta = jnp.einsum('bnqkgd,bmkd->bnkgqm', qb, km).astype(jnp.float32)
    s_meta = jnp.where(meta_ok[None, :, None, None], s_meta, -jnp.inf)
    sink = jnp.broadcast_to(sinks.astype(jnp.float32).reshape(1, 1, ATT_KV_HEADS, ATT_GROUP, 1, 1),
                            s_meta.shape[:-1] + (1,))
    p = jax.nn.softmax(jnp.concatenate([sink, s_meta, s_band], axis=-1), axis=-1).astype(v.dtype)
    p_meta, p_band = p[..., 1:1 + N_META], p[..., 1 + N_META:]
    o = jnp.einsum('bnkgqm,bmkd->bnqkgd', p_meta, vm) + jnp.einsum('bnkgqs,bnskd->bnqkgd', p_band, vb)
    return o.reshape(b, lp, ATT_WIDTH)


def causal_depthwise_conv(u, w, bias):
    out = lax.conv_general_dilated(u, w[:, None, :].astype(u.dtype), window_strides=(1,),
                                   padding=[(CONV_WIDTH - 1, 0)],
                                   dimension_numbers=('NWC', 'WIO', 'NWC'),
                                   feature_group_count=u.shape[-1])
    return out + bias.astype(u.dtype)


def segsum(a):
    T = a.shape[-1]
    cs = jnp.cumsum(a, axis=-1)
    diff = cs[..., :, None] - cs[..., None, :]
    return jnp.where(jnp.tril(jnp.ones((T, T), bool)), diff, -jnp.inf)


def ssd_chunked(x, dt, A, Bm, Cm):
    b, L = x.shape[:2]
    nc = L // BLOCK
    G, R, P, N = SSM_GROUPS, SSM_HEADS_PER_GROUP, SSM_HEAD_DIM, SSM_STATE
    xr = (x * dt[..., None]).reshape(b, nc, BLOCK, G, R, P)
    a = (dt * A).reshape(b, nc, BLOCK, G, R).transpose(0, 1, 3, 4, 2)
    Br = Bm.reshape(b, nc, BLOCK, G, N)
    Cr = Cm.reshape(b, nc, BLOCK, G, N)
    a_cs = jnp.cumsum(a, axis=-1)
    decay = jnp.exp(segsum(a))
    cb = jnp.einsum('bclgn,bcsgn->bcgls', Cr, Br)
    y_diag = jnp.einsum('bcgls,bcgrls,bcsgrp->bclgrp', cb, decay, xr)
    decay_states = jnp.exp(a_cs[..., -1:] - a_cs)
    states = jnp.einsum('bclgn,bcgrl,bclgrp->bcgrpn', Br, decay_states, xr)
    chunk_decay = jnp.exp(a_cs[..., -1])

    def step(carry, inp):
        s_c, d_c = inp
        return carry * d_c[..., None, None] + s_c, carry

    init = jnp.zeros((b, G, R, P, N), jnp.float32)
    _, prev = lax.scan(step, init, (jnp.moveaxis(states, 1, 0), jnp.moveaxis(chunk_decay, 1, 0)))
    prev = jnp.moveaxis(prev, 0, 1)
    y_off = jnp.einsum('bclgn,bcgrpn,bcgrl->bclgrp', Cr, prev, jnp.exp(a_cs))
    return (y_diag + y_off).reshape(b, L, SSM_HEADS, P)


def ssd_branch(z, xbc, dt_raw, conv_w, conv_b, dt_bias, a_log, d_skip, g_norm, valid):
    b, L, _ = xbc.shape
    xbc = jax.nn.silu(causal_depthwise_conv(xbc, conv_w, conv_b)) * valid[None, :, None]
    xbc = xbc.astype(jnp.float32)
    xs = xbc[..., :SSM_INNER].reshape(b, L, SSM_HEADS, SSM_HEAD_DIM)
    Bm = xbc[..., SSM_INNER:SSM_INNER + SSM_GROUPS * SSM_STATE].reshape(b, L, SSM_GROUPS, SSM_STATE)
    Cm = xbc[..., SSM_INNER + SSM_GROUPS * SSM_STATE:].reshape(b, L, SSM_GROUPS, SSM_STATE)
    dt = jax.nn.softplus(dt_raw.astype(jnp.float32) + dt_bias.astype(jnp.float32))
    A = -jnp.exp(a_log.astype(jnp.float32))
    y = ssd_chunked(xs, dt, A, Bm, Cm) + d_skip.astype(jnp.float32)[:, None] * xs
    y = y.reshape(b, L, SSM_INNER) * jax.nn.silu(z.astype(jnp.float32))
    yg = y.reshape(b, L, SSM_GROUPS, SSM_INNER // SSM_GROUPS)
    yg = yg * lax.rsqrt(jnp.mean(yg * yg, axis=-1, keepdims=True) + NORM_EPS)
    y = yg.reshape(b, L, SSM_INNER) * g_norm.astype(jnp.float32)
    return y.astype(z.dtype)


def hybrid_layer(h, valid, g_pre, w_in, conv_w, conv_b, dt_bias, a_log, d_skip, attn_sinks,
                 g_ssm_norm, w_out_att, w_out_ssm, w_out, g_post):
    u = rmsnorm(h, g_pre)
    proj = u @ w_in
    q, k, v, z_att, z_ssm, xbc, dt_raw, gate_att, gate_ssm = jnp.split(proj, SPLIT_POINTS, axis=-1)
    y_att = (sliding_window_attention(q, k, v, attn_sinks) * jax.nn.silu(z_att)) @ w_out_att
    y_ssm = ssd_branch(z_ssm, xbc, dt_raw, conv_w, conv_b, dt_bias, a_log, d_skip, g_ssm_norm, valid) @ w_out_ssm
    merged = jax.nn.sigmoid(gate_att) * y_att + jax.nn.sigmoid(gate_ssm) * y_ssm
    out = merged @ w_out
    return h + rmsnorm(out, g_post) * valid[None, :, None]


def setup_inputs(seed: int = 0) -> dict:
    key = jax.random.key(seed)
    ks = jax.random.split(key, 16)
    f32 = jnp.float32

    def nrm(k, shape, scale):
        return jax.random.normal(k, shape, f32) * scale

    x = nrm(ks[0], (BATCH, SEQ, D_MODEL), 1.0)
    meta_tokens = nrm(ks[1], (N_META, D_MODEL), 1.0)
    g_pre = 1.0 + nrm(ks[2], (DEPTH, D_MODEL), 0.01)
    w_in = nrm(ks[3], (DEPTH, D_MODEL, IN_PROJ_DIM), D_MODEL ** -0.5)
    conv_w = nrm(ks[4], (DEPTH, CONV_WIDTH, CONV_DIM), CONV_WIDTH ** -0.5)
    conv_b = nrm(ks[5], (DEPTH, CONV_DIM), 0.02)
    dt0 = jnp.exp(jax.random.uniform(ks[6], (DEPTH, SSM_HEADS), f32, math.log(1e-3), math.log(1e-1)))
    dt_bias = dt0 + jnp.log(-jnp.expm1(-dt0))
    a_log = jnp.log(jax.random.uniform(ks[7], (DEPTH, SSM_HEADS), f32, 1.0, 16.0))
    d_skip = 1.0 + nrm(ks[8], (DEPTH, SSM_HEADS), 0.1)
    attn_sinks = nrm(ks[9], (DEPTH, ATT_Q_HEADS), 0.5)
    g_ssm_norm = 1.0 + nrm(ks[10], (DEPTH, SSM_INNER), 0.01)
    w_out_att = nrm(ks[11], (DEPTH, ATT_WIDTH, D_MODEL), ATT_WIDTH ** -0.5)
    w_out_ssm = nrm(ks[12], (DEPTH, SSM_INNER, D_MODEL), SSM_INNER ** -0.5)
    w_out = nrm(ks[13], (DEPTH, D_MODEL, D_MODEL), D_MODEL ** -0.5)
    g_post = 1.0 + nrm(ks[14], (DEPTH, D_MODEL), 0.01)
    return {'x': x, 'meta_tokens': meta_tokens, 'g_pre': g_pre, 'w_in': w_in, 'conv_w': conv_w,
            'conv_b': conv_b, 'dt_bias': dt_bias, 'a_log': a_log, 'd_skip': d_skip,
            'attn_sinks': attn_sinks, 'g_ssm_norm': g_ssm_norm, 'w_out_att': w_out_att,
            'w_out_ssm': w_out_ssm, 'w_out': w_out, 'g_post': g_post}


def reference(x, meta_tokens, g_pre, w_in, conv_w, conv_b, dt_bias, a_log, d_skip, attn_sinks,
              g_ssm_norm, w_out_att, w_out_ssm, w_out, g_post):
    b = x.shape[0]
    lp = PAD + N_META + x.shape[1]
    h = jnp.concatenate([jnp.zeros((b, PAD, D_MODEL), x.dtype),
                         jnp.broadcast_to(meta_tokens.astype(x.dtype)[None], (b, N_META, D_MODEL)),
                         x], axis=1)
    valid = (jnp.arange(lp) >= PAD).astype(x.dtype)
    for i in range(DEPTH):
        h = hybrid_layer(h, valid, g_pre[i], w_in[i], conv_w[i], conv_b[i], dt_bias[i], a_log[i],
                         d_skip[i], attn_sinks[i], g_ssm_norm[i], w_out_att[i], w_out_ssm[i],
                         w_out[i], g_post[i])
    return h[:, PAD + N_META:]
```

```python
import functools
import math

import jax
import jax.numpy as jnp
from jax import lax
from jax.experimental import pallas as pl
from jax.experimental.pallas import tpu as pltpu

F32 = jnp.float32
BF16 = jnp.bfloat16

D_MODEL = 1024
N_META = 16
BLOCK = 128
PAD = (-N_META) % BLOCK
NORM_EPS = 1e-6

ATT_HEAD_DIM = 64
ATT_Q_HEADS = D_MODEL // ATT_HEAD_DIM
ATT_KV_HEADS = 4
ATT_GROUP = ATT_Q_HEADS // ATT_KV_HEADS
ATT_WIDTH = ATT_Q_HEADS * ATT_HEAD_DIM
KV_WIDTH = ATT_KV_HEADS * ATT_HEAD_DIM

SSM_INNER = 2 * D_MODEL
SSM_HEAD_DIM = 64
SSM_HEADS = SSM_INNER // SSM_HEAD_DIM
SSM_GROUPS = 4
SSM_HEADS_PER_GROUP = SSM_HEADS // SSM_GROUPS
SSM_STATE = 128
CONV_WIDTH = 4
CONV_DIM = SSM_INNER + 2 * SSM_GROUPS * SSM_STATE

SPLIT_SIZES = (ATT_WIDTH, KV_WIDTH, KV_WIDTH, ATT_WIDTH, SSM_INNER, CONV_DIM, SSM_HEADS, D_MODEL, D_MODEL)

LANES = 128
CONV_TAIL_ROWS = 8

PACK_WIDTH = ATT_WIDTH + ATT_WIDTH + SSM_INNER + 2 * D_MODEL + CONV_DIM + 2 * KV_WIDTH
Q_BLK = 0
ZATT_BLK = 1
ZSSM_BLK = 1
GATE_BLK = 2
XBC_BLK = 2
K_BLK = (PACK_WIDTH - 2 * KV_WIDTH) // KV_WIDTH
V_BLK = K_BLK + 1

VMEM_LIMIT = 48 * 1024 * 1024


def _sigmoid(x):
    return 1.0 / (1.0 + jnp.exp(-x))


def _silu(x):
    return x * _sigmoid(x)


def _inproj_kernel(h_ref, g_ref, w_ref, wdt_ref, proj_ref, dt_ref, u_scr):
    @pl.when(pl.program_id(1) == 0)
    def _():
        h = h_ref[...]
        ms = jnp.mean(h * h, axis=-1, keepdims=True)
        u = (h * lax.rsqrt(ms + NORM_EPS) * g_ref[...]).astype(BF16)
        u_scr[...] = u
        dt_ref[...] = jnp.dot(u, wdt_ref[...], preferred_element_type=F32)

    proj_ref[...] = jnp.dot(u_scr[...], w_ref[...], preferred_element_type=F32).astype(BF16)


def _inproj(h2, g_pre, w_pack, w_dt, *, tm, n_col_steps):
    rows = h2.shape[0]
    tn = PACK_WIDTH // n_col_steps
    return pl.pallas_call(
        _inproj_kernel,
        grid=(rows // tm, n_col_steps),
        in_specs=[
            pl.BlockSpec((tm, D_MODEL), lambda i, j: (i, 0)),
            pl.BlockSpec((1, D_MODEL), lambda i, j: (0, 0)),
            pl.BlockSpec((D_MODEL, tn), lambda i, j: (0, j)),
            pl.BlockSpec((D_MODEL, LANES), lambda i, j: (0, 0)),
        ],
        out_specs=[
            pl.BlockSpec((tm, tn), lambda i, j: (i, j)),
            pl.BlockSpec((tm, LANES), lambda i, j: (i, 0)),
        ],
        out_shape=[
            jax.ShapeDtypeStruct((rows, PACK_WIDTH), BF16),
            jax.ShapeDtypeStruct((rows, LANES), F32),
        ],
        scratch_shapes=[pltpu.VMEM((tm, D_MODEL), BF16)],
        compiler_params=pltpu.CompilerParams(
            dimension_semantics=("arbitrary", "arbitrary"), vmem_limit_bytes=VMEM_LIMIT),
        name="in_proj",
    )(h2, g_pre, w_pack, w_dt)


def _attn_kernel(sink_ref, q_ref, kc_ref, vc_ref, kp_ref, vp_ref, km_ref, vm_ref, z_ref, o_ref):
    j = pl.program_id(1)
    row = lax.broadcasted_iota(jnp.int32, (BLOCK, BLOCK), 0)
    col = lax.broadcasted_iota(jnp.int32, (BLOCK, BLOCK), 1)
    in_cur = col <= row
    rel = jnp.where(in_cur, row - col, row - col + BLOCK).astype(F32)
    has_prev = j > 0

    q = q_ref[0]
    k_band = jnp.concatenate([kp_ref[0], kc_ref[0]], axis=0)
    v_band = jnp.concatenate([vp_ref[0], vc_ref[0]], axis=0)
    k_meta = km_ref[0]
    v_meta = vm_ref[0]
    nt = (((1,), (1,)), ((), ()))
    outs = []
    for h in range(ATT_Q_HEADS):
        g = h // ATT_GROUP
        hs = slice(h * ATT_HEAD_DIM, (h + 1) * ATT_HEAD_DIM)
        gs = slice(g * ATT_HEAD_DIM, (g + 1) * ATT_HEAD_DIM)
        slope = 2.0 ** (-8.0 * (h + 1) / ATT_Q_HEADS)
        sink = sink_ref[h]
        qh = q[:, hs] * (ATT_HEAD_DIM ** -0.5)
        s_band = lax.dot_general(qh, k_band[:, gs], nt, preferred_element_type=F32)
        s_meta = lax.dot_general(qh, k_meta[:, gs], nt, preferred_element_type=F32)
        s_prev = jnp.where(has_prev, s_band[:, :BLOCK], -jnp.inf)
        s = jnp.where(in_cur, s_band[:, BLOCK:], s_prev) - slope * rel
        m = jnp.maximum(jnp.max(s, axis=1, keepdims=True), jnp.max(s_meta, axis=1, keepdims=True))
        m = jnp.maximum(m, sink)
        p = jnp.exp(s - m)
        p_meta = jnp.exp(s_meta - m)
        denom = (jnp.sum(p, axis=1, keepdims=True) + jnp.sum(p_meta, axis=1, keepdims=True)
                 + jnp.exp(sink - m))
        p_band = jnp.concatenate([jnp.where(in_cur, 0.0, p), jnp.where(in_cur, p, 0.0)], axis=1)
        o = jnp.dot(p_band.astype(BF16), v_band[:, gs], preferred_element_type=F32)
        o = o + jnp.dot(p_meta.astype(BF16), v_meta[:, gs], preferred_element_type=F32)
        outs.append(o / denom)
    o_all = jnp.concatenate(outs, axis=1)
    o_ref[0] = (o_all * _silu(z_ref[0].astype(F32))).astype(BF16)


def _attention(proj3, proj_meta3, sinks):
    b, seq, _ = proj3.shape
    nb = seq // BLOCK
    meta_row_blk = PAD // N_META
    return pl.pallas_call(
        _attn_kernel,
        grid=(b, nb),
        in_specs=[
            pl.BlockSpec(memory_space=pltpu.SMEM),
            pl.BlockSpec((1, BLOCK, ATT_WIDTH), lambda i, j: (i, j, Q_BLK)),
            pl.BlockSpec((1, BLOCK, KV_WIDTH), lambda i, j: (i, j, K_BLK)),
            pl.BlockSpec((1, BLOCK, KV_WIDTH), lambda i, j: (i, j, V_BLK)),
            pl.BlockSpec((1, BLOCK, KV_WIDTH), lambda i, j: (i, jnp.maximum(j - 1, 0), K_BLK)),
            pl.BlockSpec((1, BLOCK, KV_WIDTH), lambda i, j: (i, jnp.maximum(j - 1, 0), V_BLK)),
            pl.BlockSpec((1, N_META, KV_WIDTH), lambda i, j: (0, meta_row_blk, K_BLK)),
            pl.BlockSpec((1, N_META, KV_WIDTH), lambda i, j: (0, meta_row_blk, V_BLK)),
            pl.BlockSpec((1, BLOCK, ATT_WIDTH), lambda i, j: (i, j, ZATT_BLK)),
        ],
        out_specs=pl.BlockSpec((1, BLOCK, ATT_WIDTH), lambda i, j: (i, j, 0)),
        out_shape=jax.ShapeDtypeStruct((b, seq, ATT_WIDTH), BF16),
        compiler_params=pltpu.CompilerParams(
            dimension_semantics=("arbitrary", "arbitrary"), vmem_limit_bytes=VMEM_LIMIT),
        name="swa_attention",
    )(sinks, proj3, proj3, proj3, proj3, proj3, proj_meta3, proj_meta3, proj3)


def _split3_bf16(a):
    hi = a.astype(BF16)
    r1 = a - hi.astype(F32)
    mid = r1.astype(BF16)
    lo = (r1 - mid.astype(F32)).astype(BF16)
    return hi, mid, lo


def _ssd_kernel(z_ref, xbc_ref, dt_ref, convw_ref, convb_ref, dtb_ref, alog_ref, dskip_ref, gnorm_ref,
                s0_ref, tail0_ref, *rest, first_valid_row, emit_y, emit_state):
    rest = list(rest)
    y_ref = rest.pop(0) if emit_y else None
    if emit_state:
        sout_ref = rest.pop(0)
        tailout_ref = rest.pop(0)
    state_scr, ext_scr, y_scr, csrow_scr, dtrow_scr = rest

    c = pl.program_id(1)

    @pl.when(c == 0)
    def _():
        state_scr[...] = s0_ref[...]
        ext_scr[0:CONV_TAIL_ROWS, :] = tail0_ref[...]

    xraw = xbc_ref[0].astype(F32)
    ext_scr[CONV_TAIL_ROWS:, :] = xraw
    conv = convb_ref[...]
    for k in range(CONV_WIDTH):
        off = CONV_TAIL_ROWS - (CONV_WIDTH - 1) + k
        conv = conv + convw_ref[k:k + 1, :] * ext_scr[off:off + BLOCK, :]
    new_tail = xraw[BLOCK - CONV_TAIL_ROWS:, :]
    ext_scr[0:CONV_TAIL_ROWS, :] = new_tail
    xact = _silu(conv)
    if first_valid_row:
        rows = lax.broadcasted_iota(jnp.int32, (BLOCK, 1), 0) + c * BLOCK
        xact = jnp.where(rows >= first_valid_row, xact, 0.0)
    n_bc = SSM_GROUPS * SSM_STATE
    xs = xact[:, :SSM_INNER]
    xs_b = xs.astype(BF16)
    b_all = xact[:, SSM_INNER:SSM_INNER + n_bc]
    c_all = xact[:, SSM_INNER + n_bc:]

    x_dt = dt_ref[0] + dtb_ref[...]
    dt = jnp.maximum(x_dt, 0.0) + jnp.log1p(jnp.exp(-jnp.abs(x_dt)))
    a = dt * (-jnp.exp(alog_ref[...]))
    row = lax.broadcasted_iota(jnp.int32, (BLOCK, BLOCK), 0)
    col = lax.broadcasted_iota(jnp.int32, (BLOCK, BLOCK), 1)
    tril = row >= col
    tril_b = tril.astype(BF16)
    cs_col = sum(jnp.dot(tril_b, part, preferred_element_type=F32) for part in _split3_bf16(a))
    csrow_scr[...] = cs_col.T
    dtrow_scr[...] = dt.T
    lane = lax.broadcasted_iota(jnp.int32, (1, LANES), 1)
    low_half = lane < SSM_HEAD_DIM
    nt = (((1,), (1,)), ((), ()))

    for g in range(SSM_GROUPS):
        bg = b_all[:, g * SSM_STATE:(g + 1) * SSM_STATE]
        cg = c_all[:, g * SSM_STATE:(g + 1) * SSM_STATE]
        cb = lax.dot_general(cg.astype(BF16), bg.astype(BF16), nt, preferred_element_type=F32)
        bg_t = bg.T
        for pair in range(SSM_HEADS_PER_GROUP // 2):
            h0 = g * SSM_HEADS_PER_GROUP + 2 * pair
            cols = slice(h0 * SSM_HEAD_DIM, (h0 + 2) * SSM_HEAD_DIM)
            m_parts, ce_parts, bw_parts, cd_parts = [], [], [], []
            for h in (h0, h0 + 1):
                cs_l = cs_col[:, h:h + 1]
                cs_s = csrow_scr[h:h + 1, :]
                dt_s = dtrow_scr[h:h + 1, :]
                cs_last = cs_s[:, BLOCK - 1:BLOCK]
                decay = jnp.exp(jnp.where(tril, cs_l - cs_s, -jnp.inf))
                m_parts.append((cb * decay * dt_s).astype(BF16))
                ce_parts.append((cg * jnp.exp(cs_l)).astype(BF16))
                bw_parts.append((bg_t * (jnp.exp(cs_last - cs_s) * dt_s)).astype(BF16))
                cd_parts.append(jnp.exp(cs_last))
            x_pair = xs_b[:, cols]
            x_bd = jnp.concatenate([jnp.where(low_half, x_pair, 0), jnp.where(low_half, 0, x_pair)], axis=0)
            s_prev = state_scr[:, cols]
            s_prev_b = s_prev.astype(BF16)
            s_bd = jnp.concatenate([jnp.where(low_half, s_prev_b, 0), jnp.where(low_half, 0, s_prev_b)], axis=0)
            y_pair = jnp.dot(jnp.concatenate(m_parts, axis=1), x_bd, preferred_element_type=F32)
            y_pair = y_pair + jnp.dot(jnp.concatenate(ce_parts, axis=1), s_bd, preferred_element_type=F32)
            new_states = jnp.dot(jnp.concatenate(bw_parts, axis=1), x_bd, preferred_element_type=F32)
            chunk_decay = jnp.where(low_half, cd_parts[0], cd_parts[1])
            state_scr[:, cols] = s_prev * chunk_decay + new_states
            if emit_y:
                y_scr[:, cols] = y_pair + dskip_ref[:, cols] * xs[:, cols]

    if emit_y:
        y = y_scr[...] * _silu(z_ref[0].astype(F32))
        gw = SSM_INNER // SSM_GROUPS
        normed = []
        for g in range(SSM_GROUPS):
            yg = y[:, g * gw:(g + 1) * gw]
            normed.append(yg * lax.rsqrt(jnp.mean(yg * yg, axis=-1, keepdims=True) + NORM_EPS))
        y_ref[0] = (jnp.concatenate(normed, axis=1) * gnorm_ref[...]).astype(BF16)

    if emit_state:
        @pl.when(c == pl.num_programs(1) - 1)
        def _():
            sout_ref[...] = state_scr[...]
            tailout_ref[...] = new_tail


def _ssd(proj3, dt3, conv_w, conv_b, dt_bias, a_log, d_skip, g_norm, s0, tail0, *,
         first_valid_row, emit_y, emit_state):
    b, seq, _ = proj3.shape
    nc = seq // BLOCK
    full = lambda shape: pl.BlockSpec(shape, lambda i, j: (0,) * len(shape))
    out_specs, out_shape = [], []
    if emit_y:
        out_specs.append(pl.BlockSpec((1, BLOCK, SSM_INNER), lambda i, j: (i, j, 0)))
        out_shape.append(jax.ShapeDtypeStruct((b, seq, SSM_INNER), BF16))
    if emit_state:
        assert b == 1
        out_specs += [full((SSM_STATE, SSM_INNER)), full((CONV_TAIL_ROWS, CONV_DIM))]
        out_shape += [jax.ShapeDtypeStruct((SSM_STATE, SSM_INNER), F32),
                      jax.ShapeDtypeStruct((CONV_TAIL_ROWS, CONV_DIM), F32)]
    return pl.pallas_call(
        functools.partial(_ssd_kernel, first_valid_row=first_valid_row, emit_y=emit_y, emit_state=emit_state),
        grid=(b, nc),
        in_specs=[
            pl.BlockSpec((1, BLOCK, SSM_INNER), lambda i, j: (i, j, ZSSM_BLK)),
            pl.BlockSpec((1, BLOCK, CONV_DIM), lambda i, j: (i, j, XBC_BLK)),
            pl.BlockSpec((1, BLOCK, LANES), lambda i, j: (i, j, 0)),
            full((CONV_WIDTH, CONV_DIM)),
            full((1, CONV_DIM)),
            full((1, LANES)),
            full((1, LANES)),
            full((1, SSM_INNER)),
            full((1, SSM_INNER)),
            full((SSM_STATE, SSM_INNER)),
            full((CONV_TAIL_ROWS, CONV_DIM)),
        ],
        out_specs=out_specs,
        out_shape=out_shape,
        scratch_shapes=[
            pltpu.VMEM((SSM_STATE, SSM_INNER), F32),
            pltpu.VMEM((CONV_TAIL_ROWS + BLOCK, CONV_DIM), F32),
            pltpu.VMEM((BLOCK, SSM_INNER), F32),
            pltpu.VMEM((LANES, BLOCK), F32),
            pltpu.VMEM((LANES, BLOCK), F32),
        ],
        compiler_params=pltpu.CompilerParams(
            dimension_semantics=("arbitrary", "arbitrary"), vmem_limit_bytes=VMEM_LIMIT),
        name="ssd",
    )(proj3, proj3, dt3, conv_w, conv_b, dt_bias, a_log, d_skip, g_norm, s0, tail0)


def _out_kernel(ya_ref, ys_ref, gate_ref, x_ref, woa_ref, wos_ref, wo_ref, gpost_ref, o_ref):
    ya = jnp.dot(ya_ref[...], woa_ref[...], preferred_element_type=F32)
    ys = jnp.dot(ys_ref[...], wos_ref[...], preferred_element_type=F32)
    gates = _sigmoid(gate_ref[...].astype(F32))
    merged = gates[:, :D_MODEL] * ya + gates[:, D_MODEL:] * ys
    out = jnp.dot(merged.astype(BF16), wo_ref[...], preferred_element_type=F32)
    ms = jnp.mean(out * out, axis=-1, keepdims=True)
    o_ref[...] = x_ref[...] + out * lax.rsqrt(ms + NORM_EPS) * gpost_ref[...]


def _merge_out(y_att, y_ssm, proj, x2, w_out_att, w_out_ssm, w_out, g_post, *, tm):
    rows = x2.shape[0]
    full = lambda shape: pl.BlockSpec(shape, lambda i: (0,) * len(shape))
    return pl.pallas_call(
        _out_kernel,
        grid=(rows // tm,),
        in_specs=[
            pl.BlockSpec((tm, ATT_WIDTH), lambda i: (i, 0)),
            pl.BlockSpec((tm, SSM_INNER), lambda i: (i, 0)),
            pl.BlockSpec((tm, 2 * D_MODEL), lambda i: (i, GATE_BLK)),
            pl.BlockSpec((tm, D_MODEL), lambda i: (i, 0)),
            full((ATT_WIDTH, D_MODEL)),
            full((SSM_INNER, D_MODEL)),
            full((D_MODEL, D_MODEL)),
            full((1, D_MODEL)),
        ],
        out_specs=pl.BlockSpec((tm, D_MODEL), lambda i: (i, 0)),
        out_shape=jax.ShapeDtypeStruct((rows, D_MODEL), F32),
        compiler_params=pltpu.CompilerParams(
            dimension_semantics=("arbitrary",), vmem_limit_bytes=VMEM_LIMIT),
        name="merge_out",
    )(y_att, y_ssm, proj, x2, w_out_att, w_out_ssm, w_out, g_post)


def _pack_in_proj(w_in):
    q, k, v, z_att, z_ssm, xbc, dt, gate_att, gate_ssm = _split_cols(w_in)
    w_pack = jnp.concatenate([q, z_att, z_ssm, gate_att, gate_ssm, xbc, k, v], axis=1).astype(BF16)
    w_dt = jnp.pad(dt, ((0, 0), (0, LANES - SSM_HEADS))).astype(BF16)
    return w_pack, w_dt


def _split_cols(w):
    parts, start = [], 0
    for size in SPLIT_SIZES:
        parts.append(w[:, start:start + size])
        start += size
    return parts


def _pad_lanes(v):
    return jnp.pad(v.astype(F32), (0, LANES - v.shape[0])).reshape(1, LANES)


def kernel(x, meta_tokens, g_pre, w_in, conv_w, conv_b, dt_bias, a_log, d_skip, attn_sinks, g_ssm_norm,
           w_out_att, w_out_ssm, w_out, g_post):
    b, seq, d = x.shape
    assert d == D_MODEL and seq % BLOCK == 0 and PAD + N_META == BLOCK
    assert g_pre.shape[0] == 1, "one layer"

    w_pack, w_dt = _pack_in_proj(w_in[0])
    g_pre2 = g_pre[0].reshape(1, D_MODEL)
    conv_b2 = conv_b[0].reshape(1, CONV_DIM)
    dt_bias2 = _pad_lanes(dt_bias[0])
    a_log2 = _pad_lanes(a_log[0])
    d_skip2 = jnp.repeat(d_skip[0].astype(F32), SSM_HEAD_DIM).reshape(1, SSM_INNER)
    g_norm2 = g_ssm_norm[0].reshape(1, SSM_INNER)
    ssd_params = (conv_w[0], conv_b2, dt_bias2, a_log2, d_skip2, g_norm2)

    h_meta = jnp.concatenate([jnp.zeros((PAD, D_MODEL), x.dtype), meta_tokens.astype(x.dtype)], axis=0)
    proj_m, dt_m = _inproj(h_meta, g_pre2, w_pack, w_dt, tm=BLOCK, n_col_steps=4)
    proj_m3 = proj_m.reshape(1, BLOCK, PACK_WIDTH)
    state0, tail0 = _ssd(proj_m3, dt_m.reshape(1, BLOCK, LANES), *ssd_params,
                         jnp.zeros((SSM_STATE, SSM_INNER), F32), jnp.zeros((CONV_TAIL_ROWS, CONV_DIM), F32),
                         first_valid_row=PAD, emit_y=False, emit_state=True)

    x2 = x.reshape(b * seq, D_MODEL)
    proj, dt = _inproj(x2, g_pre2, w_pack, w_dt, tm=512, n_col_steps=4)
    proj3 = proj.reshape(b, seq, PACK_WIDTH)
    y_att = _attention(proj3, proj_m3, attn_sinks[0].astype(F32))
    (y_ssm,) = _ssd(proj3, dt.reshape(b, seq, LANES), *ssd_params, state0, tail0,
                    first_valid_row=0, emit_y=True, emit_state=False)
    out = _merge_out(y_att.reshape(b * seq, ATT_WIDTH), y_ssm.reshape(b * seq, SSM_INNER), proj, x2,
                     w_out_att[0].astype(BF16), w_out_ssm[0].astype(BF16), w_out[0].astype(BF16),
                     g_post[0].reshape(1, D_MODEL), tm=512)
    return out.reshape(b, seq, D_MODEL)
```

```python
import functools
import math

import jax
import jax.numpy as jnp
from jax import lax
from jax.experimental import pallas as pl
from jax.experimental.pallas import tpu as pltpu

F32 = jnp.float32
BF16 = jnp.bfloat16

D_MODEL = 1024
N_META = 16
BLOCK = 128
PAD = (-N_META) % BLOCK
NORM_EPS = 1e-6

ATT_HEAD_DIM = 64
ATT_Q_HEADS = D_MODEL // ATT_HEAD_DIM
ATT_KV_HEADS = 4
ATT_GROUP = ATT_Q_HEADS // ATT_KV_HEADS
ATT_WIDTH = ATT_Q_HEADS * ATT_HEAD_DIM
KV_WIDTH = ATT_KV_HEADS * ATT_HEAD_DIM

SSM_INNER = 2 * D_MODEL
SSM_HEAD_DIM = 64
SSM_HEADS = SSM_INNER // SSM_HEAD_DIM
SSM_GROUPS = 4
SSM_HEADS_PER_GROUP = SSM_HEADS // SSM_GROUPS
SSM_STATE = 128
CONV_WIDTH = 4
CONV_DIM = SSM_INNER + 2 * SSM_GROUPS * SSM_STATE

SPLIT_SIZES = (ATT_WIDTH, KV_WIDTH, KV_WIDTH, ATT_WIDTH, SSM_INNER, CONV_DIM, SSM_HEADS, D_MODEL, D_MODEL)

LANES = 128
CONV_TAIL_ROWS = 8

PACK_WIDTH = ATT_WIDTH + ATT_WIDTH + SSM_INNER + 2 * D_MODEL + CONV_DIM + 2 * KV_WIDTH
Q_BLK = 0
ZATT_BLK = 1
ZSSM_BLK = 1
GATE_BLK = 2
XBC_BLK = 2
K_BLK = (PACK_WIDTH - 2 * KV_WIDTH) // KV_WIDTH
V_BLK = K_BLK + 1

VMEM_LIMIT = 48 * 1024 * 1024


def _sigmoid(x):
    return 1.0 / (1.0 + jnp.exp(-x))


def _silu(x):
    return x * _sigmoid(x)


def _inproj_kernel(h_ref, g_ref, w_ref, wdt_ref, proj_ref, dt_ref, u_scr):
    @pl.when(pl.program_id(1) == 0)
    def _():
        h = h_ref[...]
        ms = jnp.mean(h * h, axis=-1, keepdims=True)
        u = (h * lax.rsqrt(ms + NORM_EPS) * g_ref[...]).astype(BF16)
        u_scr[...] = u
        dt_ref[...] = jnp.dot(u, wdt_ref[...], preferred_element_type=F32)

    proj_ref[...] = jnp.dot(u_scr[...], w_ref[...], preferred_element_type=F32).astype(BF16)


def _inproj(h2, g_pre, w_pack, w_dt, *, tm, n_col_steps):
    rows = h2.shape[0]
    tn = PACK_WIDTH // n_col_steps
    return pl.pallas_call(
        _inproj_kernel,
        grid=(rows // tm, n_col_steps),
        in_specs=[
            pl.BlockSpec((tm, D_MODEL), lambda i, j: (i, 0)),
            pl.BlockSpec((1, D_MODEL), lambda i, j: (0, 0)),
            pl.BlockSpec((D_MODEL, tn), lambda i, j: (0, j)),
            pl.BlockSpec((D_MODEL, LANES), lambda i, j: (0, 0)),
        ],
        out_specs=[
            pl.BlockSpec((tm, tn), lambda i, j: (i, j)),
            pl.BlockSpec((tm, LANES), lambda i, j: (i, 0)),
        ],
        out_shape=[
            jax.ShapeDtypeStruct((rows, PACK_WIDTH), BF16),
            jax.ShapeDtypeStruct((rows, LANES), F32),
        ],
        scratch_shapes=[pltpu.VMEM((tm, D_MODEL), BF16)],
        compiler_params=pltpu.CompilerParams(
            dimension_semantics=("arbitrary", "arbitrary"), vmem_limit_bytes=VMEM_LIMIT),
        name="in_proj",
    )(h2, g_pre, w_pack, w_dt)


def _attn_kernel(sink_ref, q_ref, kc_ref, vc_ref, kp_ref, vp_ref, km_ref, vmt_ref, z_ref, o_ref):
    j = pl.program_id(1)
    n_lanes = ATT_GROUP * BLOCK
    key = lax.broadcasted_iota(jnp.int32, (BLOCK, n_lanes), 0)
    qry = lax.broadcasted_iota(jnp.int32, (BLOCK, n_lanes), 1) % BLOCK
    in_cur = key <= qry
    rel = jnp.where(in_cur, qry - key, qry - key + BLOCK).astype(F32)
    rel = jnp.where(jnp.logical_or(in_cur, j > 0), rel, jnp.inf)
    cur_b = in_cur.astype(BF16)
    prev_b = 1.0 - cur_b
    low_lanes = lax.broadcasted_iota(jnp.int32, (1, LANES), 1) < ATT_HEAD_DIM
    low_rows = lax.broadcasted_iota(jnp.int32, (LANES, 1), 0) < ATT_HEAD_DIM

    q = q_ref[0] * (ATT_HEAD_DIM ** -0.5)
    k_band = jnp.concatenate([kp_ref[0], kc_ref[0]], axis=0).astype(F32)
    v_band = jnp.concatenate([vp_ref[0], vc_ref[0]], axis=0).astype(F32)
    nt = (((1,), (1,)), ((), ()))
    outs = []
    for g in range(ATT_KV_HEADS):
        tile = slice((g // 2) * LANES, (g // 2 + 1) * LANES)
        k_tile = k_band[:, tile]
        k_swap = pltpu.roll(k_tile, ATT_HEAD_DIM, axis=1)
        k_dup = (jnp.where(low_lanes, k_tile, k_swap) if g % 2 == 0
                 else jnp.where(low_lanes, k_swap, k_tile)).astype(BF16)
        vt = v_band[:, tile].T[(g % 2) * ATT_HEAD_DIM:(g % 2 + 1) * ATT_HEAD_DIM]
        vt_dup = jnp.concatenate([vt, vt], axis=0).astype(BF16)

        q_rows = []
        for t in (2 * g, 2 * g + 1):
            q_tile = q[:, t * LANES:(t + 1) * LANES]
            q_rows += [jnp.where(low_lanes, q_tile, 0), jnp.where(low_lanes, 0, q_tile)]
        q_cat = jnp.concatenate(q_rows, axis=0)
        s_band = lax.dot_general(k_dup, q_cat, nt, preferred_element_type=F32)
        s_meta = lax.dot_general(km_ref[g], q_cat, nt, preferred_element_type=F32)

        heads = range(g * ATT_GROUP, (g + 1) * ATT_GROUP)
        slopes = jnp.concatenate(
            [jnp.full((1, BLOCK), 2.0 ** (-8.0 * (h + 1) / ATT_Q_HEADS), F32) for h in heads], axis=1)
        sinks = jnp.concatenate([jnp.full((1, BLOCK), sink_ref[h], F32) for h in heads], axis=1)
        s = jnp.where(in_cur, s_band[BLOCK:], s_band[:BLOCK]) - slopes * rel
        m = jnp.maximum(jnp.max(s, axis=0, keepdims=True), jnp.max(s_meta, axis=0, keepdims=True))
        m = jnp.maximum(m, sinks)
        p = jnp.exp(s - m)
        p_meta = jnp.exp(s_meta - m)
        denom = (jnp.sum(p, axis=0, keepdims=True) + jnp.sum(p_meta, axis=0, keepdims=True)
                 + jnp.exp(sinks - m))
        p_b = p.astype(BF16)
        p_band = jnp.concatenate([p_b * prev_b, p_b * cur_b], axis=0)
        o_t = jnp.dot(vt_dup, p_band, preferred_element_type=F32)
        o_t = o_t + jnp.dot(vmt_ref[g], p_meta.astype(BF16), preferred_element_type=F32)
        o_t = o_t * (1.0 / denom)
        for pair in range(ATT_GROUP // 2):
            head_a = o_t[:, (2 * pair) * BLOCK:(2 * pair + 1) * BLOCK]
            head_b = o_t[:, (2 * pair + 1) * BLOCK:(2 * pair + 2) * BLOCK]
            outs.append(jnp.where(low_rows, head_a, head_b).T)
    o_all = jnp.concatenate(outs, axis=1)
    o_ref[0] = (o_all * _silu(z_ref[0].astype(F32))).astype(BF16)


def _attention(proj3, k_meta_dup, v_meta_t_dup, sinks):
    b, seq, _ = proj3.shape
    nb = seq // BLOCK
    full = lambda shape: pl.BlockSpec(shape, lambda i, j: (0,) * len(shape))
    return pl.pallas_call(
        _attn_kernel,
        grid=(b, nb),
        in_specs=[
            pl.BlockSpec(memory_space=pltpu.SMEM),
            pl.BlockSpec((1, BLOCK, ATT_WIDTH), lambda i, j: (i, j, Q_BLK)),
            pl.BlockSpec((1, BLOCK, KV_WIDTH), lambda i, j: (i, j, K_BLK)),
            pl.BlockSpec((1, BLOCK, KV_WIDTH), lambda i, j: (i, j, V_BLK)),
            pl.BlockSpec((1, BLOCK, KV_WIDTH), lambda i, j: (i, jnp.maximum(j - 1, 0), K_BLK)),
            pl.BlockSpec((1, BLOCK, KV_WIDTH), lambda i, j: (i, jnp.maximum(j - 1, 0), V_BLK)),
            full((ATT_KV_HEADS, N_META, LANES)),
            full((ATT_KV_HEADS, LANES, N_META)),
            pl.BlockSpec((1, BLOCK, ATT_WIDTH), lambda i, j: (i, j, ZATT_BLK)),
        ],
        out_specs=pl.BlockSpec((1, BLOCK, ATT_WIDTH), lambda i, j: (i, j, 0)),
        out_shape=jax.ShapeDtypeStruct((b, seq, ATT_WIDTH), BF16),
        compiler_params=pltpu.CompilerParams(
            dimension_semantics=("arbitrary", "arbitrary"), vmem_limit_bytes=VMEM_LIMIT),
        name="swa_attention",
    )(sinks, proj3, proj3, proj3, proj3, proj3, k_meta_dup, v_meta_t_dup, proj3)


def _meta_kv(proj_meta):
    k_off, v_off = K_BLK * KV_WIDTH, V_BLK * KV_WIDTH
    k_dup, vt_dup = [], []
    for g in range(ATT_KV_HEADS):
        kg = proj_meta[PAD:, k_off + g * ATT_HEAD_DIM:k_off + (g + 1) * ATT_HEAD_DIM]
        vg = proj_meta[PAD:, v_off + g * ATT_HEAD_DIM:v_off + (g + 1) * ATT_HEAD_DIM]
        k_dup.append(jnp.concatenate([kg, kg], axis=1))
        vt_dup.append(jnp.concatenate([vg.T, vg.T], axis=0))
    return jnp.stack(k_dup), jnp.stack(vt_dup)


def _split3_bf16(a):
    hi = a.astype(BF16)
    r1 = a - hi.astype(F32)
    mid = r1.astype(BF16)
    lo = (r1 - mid.astype(F32)).astype(BF16)
    return hi, mid, lo


def _ssd_kernel(z_ref, xbc_ref, dt_ref, convw_ref, convb_ref, dtb_ref, alog_ref, dskip_ref, gnorm_ref,
                s0_ref, tail0_ref, *rest, first_valid_row, emit_y, emit_state):
    rest = list(rest)
    y_ref = rest.pop(0) if emit_y else None
    if emit_state:
        sout_ref = rest.pop(0)
        tailout_ref = rest.pop(0)
    state_scr, ext_scr, y_scr, csrow_scr, dtrow_scr = rest

    c = pl.program_id(1)

    @pl.when(c == 0)
    def _():
        state_scr[...] = s0_ref[...]
        ext_scr[0:CONV_TAIL_ROWS, :] = tail0_ref[...]

    xraw = xbc_ref[0].astype(F32)
    ext_scr[CONV_TAIL_ROWS:, :] = xraw
    conv = convb_ref[...]
    for k in range(CONV_WIDTH):
        off = CONV_TAIL_ROWS - (CONV_WIDTH - 1) + k
        conv = conv + convw_ref[k:k + 1, :] * ext_scr[off:off + BLOCK, :]
    new_tail = xraw[BLOCK - CONV_TAIL_ROWS:, :]
    ext_scr[0:CONV_TAIL_ROWS, :] = new_tail
    xact = _silu(conv)
    if first_valid_row:
        rows = lax.broadcasted_iota(jnp.int32, (BLOCK, 1), 0) + c * BLOCK
        xact = jnp.where(rows >= first_valid_row, xact, 0.0)
    n_bc = SSM_GROUPS * SSM_STATE
    xs = xact[:, :SSM_INNER]
    xs_b = xs.astype(BF16)
    b_all = xact[:, SSM_INNER:SSM_INNER + n_bc]
    c_all = xact[:, SSM_INNER + n_bc:]

    x_dt = dt_ref[0] + dtb_ref[...]
    dt = jnp.maximum(x_dt, 0.0) + jnp.log1p(jnp.exp(-jnp.abs(x_dt)))
    a = dt * (-jnp.exp(alog_ref[...]))
    row = lax.broadcasted_iota(jnp.int32, (BLOCK, BLOCK), 0)
    col = lax.broadcasted_iota(jnp.int32, (BLOCK, BLOCK), 1)
    tril = row >= col
    tril_b = tril.astype(BF16)
    cs_col = sum(jnp.dot(tril_b, part, preferred_element_type=F32) for part in _split3_bf16(a))
    csrow_scr[...] = cs_col.T
    dtrow_scr[...] = dt.T
    lane = lax.broadcasted_iota(jnp.int32, (1, LANES), 1)
    low_half = lane < SSM_HEAD_DIM
    nt = (((1,), (1,)), ((), ()))

    for g in range(SSM_GROUPS):
        bg = b_all[:, g * SSM_STATE:(g + 1) * SSM_STATE]
        cg = c_all[:, g * SSM_STATE:(g + 1) * SSM_STATE]
        cb = lax.dot_general(cg.astype(BF16), bg.astype(BF16), nt, preferred_element_type=F32)
        bg_t = bg.T
        for pair in range(SSM_HEADS_PER_GROUP // 2):
            h0 = g * SSM_HEADS_PER_GROUP + 2 * pair
            cols = slice(h0 * SSM_HEAD_DIM, (h0 + 2) * SSM_HEAD_DIM)
            m_parts, ce_parts, bw_parts, cd_parts = [], [], [], []
            for h in (h0, h0 + 1):
                cs_l = cs_col[:, h:h + 1]
                cs_s = csrow_scr[h:h + 1, :]
                dt_s = dtrow_scr[h:h + 1, :]
                cs_last = cs_s[:, BLOCK - 1:BLOCK]
                decay = jnp.exp(jnp.where(tril, cs_l - cs_s, -jnp.inf))
                m_parts.append((cb * decay * dt_s).astype(BF16))
                ce_parts.append((cg * jnp.exp(cs_l)).astype(BF16))
                bw_parts.append((bg_t * (jnp.exp(cs_last - cs_s) * dt_s)).astype(BF16))
                cd_parts.append(jnp.exp(cs_last))
            x_pair = xs_b[:, cols]
            x_bd = jnp.concatenate([jnp.where(low_half, x_pair, 0), jnp.where(low_half, 0, x_pair)], axis=0)
            s_prev = state_scr[:, cols]
            s_prev_b = s_prev.astype(BF16)
            s_bd = jnp.concatenate([jnp.where(low_half, s_prev_b, 0), jnp.where(low_half, 0, s_prev_b)], axis=0)
            y_pair = jnp.dot(jnp.concatenate(m_parts, axis=1), x_bd, preferred_element_type=F32)
            y_pair = y_pair + jnp.dot(jnp.concatenate(ce_parts, axis=1), s_bd, preferred_element_type=F32)
            new_states = jnp.dot(jnp.concatenate(bw_parts, axis=1), x_bd, preferred_element_type=F32)
            chunk_decay = jnp.where(low_half, cd_parts[0], cd_parts[1])
            state_scr[:, cols] = s_prev * chunk_decay + new_states
            if emit_y:
                y_scr[:, cols] = y_pair + dskip_ref[:, cols] * xs[:, cols]

    if emit_y:
        y = y_scr[...] * _silu(z_ref[0].astype(F32))
        gw = SSM_INNER // SSM_GROUPS
        normed = []
        for g in range(SSM_GROUPS):
            yg = y[:, g * gw:(g + 1) * gw]
            normed.append(yg * lax.rsqrt(jnp.mean(yg * yg, axis=-1, keepdims=True) + NORM_EPS))
        y_ref[0] = (jnp.concatenate(normed, axis=1) * gnorm_ref[...]).astype(BF16)

    if emit_state:
        @pl.when(c == pl.num_programs(1) - 1)
        def _():
            sout_ref[...] = state_scr[...]
            tailout_ref[...] = new_tail


def _ssd(proj3, dt3, conv_w, conv_b, dt_bias, a_log, d_skip, g_norm, s0, tail0, *,
         first_valid_row, emit_y, emit_state):
    b, seq, _ = proj3.shape
    nc = seq // BLOCK
    full = lambda shape: pl.BlockSpec(shape, lambda i, j: (0,) * len(shape))
    out_specs, out_shape = [], []
    if emit_y:
        out_specs.append(pl.BlockSpec((1, BLOCK, SSM_INNER), lambda i, j: (i, j, 0)))
        out_shape.append(jax.ShapeDtypeStruct((b, seq, SSM_INNER), BF16))
    if emit_state:
        assert b == 1
        out_specs += [full((SSM_STATE, SSM_INNER)), full((CONV_TAIL_ROWS, CONV_DIM))]
        out_shape += [jax.ShapeDtypeStruct((SSM_STATE, SSM_INNER), F32),
                      jax.ShapeDtypeStruct((CONV_TAIL_ROWS, CONV_DIM), F32)]
    return pl.pallas_call(
        functools.partial(_ssd_kernel, first_valid_row=first_valid_row, emit_y=emit_y, emit_state=emit_state),
        grid=(b, nc),
        in_specs=[
            pl.BlockSpec((1, BLOCK, SSM_INNER), lambda i, j: (i, j, ZSSM_BLK)),
            pl.BlockSpec((1, BLOCK, CONV_DIM), lambda i, j: (i, j, XBC_BLK)),
            pl.BlockSpec((1, BLOCK, LANES), lambda i, j: (i, j, 0)),
            full((CONV_WIDTH, CONV_DIM)),
            full((1, CONV_DIM)),
            full((1, LANES)),
            full((1, LANES)),
            full((1, SSM_INNER)),
            full((1, SSM_INNER)),
            full((SSM_STATE, SSM_INNER)),
            full((CONV_TAIL_ROWS, CONV_DIM)),
        ],
        out_specs=out_specs,
        out_shape=out_shape,
        scratch_shapes=[
            pltpu.VMEM((SSM_STATE, SSM_INNER), F32),
            pltpu.VMEM((CONV_TAIL_ROWS + BLOCK, CONV_DIM), F32),
            pltpu.VMEM((BLOCK, SSM_INNER), F32),
            pltpu.VMEM((LANES, BLOCK), F32),
            pltpu.VMEM((LANES, BLOCK), F32),
        ],
        compiler_params=pltpu.CompilerParams(
            dimension_semantics=("arbitrary", "arbitrary"), vmem_limit_bytes=VMEM_LIMIT),
        name="ssd",
    )(proj3, proj3, dt3, conv_w, conv_b, dt_bias, a_log, d_skip, g_norm, s0, tail0)


def _out_kernel(ya_ref, ys_ref, gate_ref, x_ref, woa_ref, wos_ref, wo_ref, gpost_ref, o_ref):
    ya = jnp.dot(ya_ref[...], woa_ref[...], preferred_element_type=F32)
    ys = jnp.dot(ys_ref[...], wos_ref[...], preferred_element_type=F32)
    gates = _sigmoid(gate_ref[...].astype(F32))
    merged = gates[:, :D_MODEL] * ya + gates[:, D_MODEL:] * ys
    out = jnp.dot(merged.astype(BF16), wo_ref[...], preferred_element_type=F32)
    ms = jnp.mean(out * out, axis=-1, keepdims=True)
    o_ref[...] = x_ref[...] + out * lax.rsqrt(ms + NORM_EPS) * gpost_ref[...]


def _merge_out(y_att, y_ssm, proj, x2, w_out_att, w_out_ssm, w_out, g_post, *, tm):
    rows = x2.shape[0]
    full = lambda shape: pl.BlockSpec(shape, lambda i: (0,) * len(shape))
    return pl.pallas_call(
        _out_kernel,
        grid=(rows // tm,),
        in_specs=[
            pl.BlockSpec((tm, ATT_WIDTH), lambda i: (i, 0)),
            pl.BlockSpec((tm, SSM_INNER), lambda i: (i, 0)),
            pl.BlockSpec((tm, 2 * D_MODEL), lambda i: (i, GATE_BLK)),
            pl.BlockSpec((tm, D_MODEL), lambda i: (i, 0)),
            full((ATT_WIDTH, D_MODEL)),
            full((SSM_INNER, D_MODEL)),
            full((D_MODEL, D_MODEL)),
            full((1, D_MODEL)),
        ],
        out_specs=pl.BlockSpec((tm, D_MODEL), lambda i: (i, 0)),
        out_shape=jax.ShapeDtypeStruct((rows, D_MODEL), F32),
        compiler_params=pltpu.CompilerParams(
            dimension_semantics=("arbitrary",), vmem_limit_bytes=VMEM_LIMIT),
        name="merge_out",
    )(y_att, y_ssm, proj, x2, w_out_att, w_out_ssm, w_out, g_post)


def _pack_in_proj(w_in):
    q, k, v, z_att, z_ssm, xbc, dt, gate_att, gate_ssm = _split_cols(w_in)
    w_pack = jnp.concatenate([q, z_att, z_ssm, gate_att, gate_ssm, xbc, k, v], axis=1).astype(BF16)
    w_dt = jnp.pad(dt, ((0, 0), (0, LANES - SSM_HEADS))).astype(BF16)
    return w_pack, w_dt


def _split_cols(w):
    parts, start = [], 0
    for size in SPLIT_SIZES:
        parts.append(w[:, start:start + size])
        start += size
    return parts


def _pad_lanes(v):
    return jnp.pad(v.astype(F32), (0, LANES - v.shape[0])).reshape(1, LANES)


def kernel(x, meta_tokens, g_pre, w_in, conv_w, conv_b, dt_bias, a_log, d_skip, attn_sinks, g_ssm_norm,
           w_out_att, w_out_ssm, w_out, g_post):
    b, seq, d = x.shape
    assert d == D_MODEL and seq % BLOCK == 0 and PAD + N_META == BLOCK
    assert g_pre.shape[0] == 1, "one layer"

    w_pack, w_dt = _pack_in_proj(w_in[0])
    g_pre2 = g_pre[0].reshape(1, D_MODEL)
    conv_b2 = conv_b[0].reshape(1, CONV_DIM)
    dt_bias2 = _pad_lanes(dt_bias[0])
    a_log2 = _pad_lanes(a_log[0])
    d_skip2 = jnp.repeat(d_skip[0].astype(F32), SSM_HEAD_DIM).reshape(1, SSM_INNER)
    g_norm2 = g_ssm_norm[0].reshape(1, SSM_INNER)
    ssd_params = (conv_w[0], conv_b2, dt_bias2, a_log2, d_skip2, g_norm2)

    h_meta = jnp.concatenate([jnp.zeros((PAD, D_MODEL), x.dtype), meta_tokens.astype(x.dtype)], axis=0)
    proj_m, dt_m = _inproj(h_meta, g_pre2, w_pack, w_dt, tm=BLOCK, n_col_steps=4)
    proj_m3 = proj_m.reshape(1, BLOCK, PACK_WIDTH)
    state0, tail0 = _ssd(proj_m3, dt_m.reshape(1, BLOCK, LANES), *ssd_params,
                         jnp.zeros((SSM_STATE, SSM_INNER), F32), jnp.zeros((CONV_TAIL_ROWS, CONV_DIM), F32),
                         first_valid_row=PAD, emit_y=False, emit_state=True)

    x2 = x.reshape(b * seq, D_MODEL)
    proj, dt = _inproj(x2, g_pre2, w_pack, w_dt, tm=512, n_col_steps=4)
    proj3 = proj.reshape(b, seq, PACK_WIDTH)
    y_att = _attention(proj3, *_meta_kv(proj_m), attn_sinks[0].astype(F32))
    (y_ssm,) = _ssd(proj3, dt.reshape(b, seq, LANES), *ssd_params, state0, tail0,
                    first_valid_row=0, emit_y=True, emit_state=False)
    out = _merge_out(y_att.reshape(b * seq, ATT_WIDTH), y_ssm.reshape(b * seq, SSM_INNER), proj, x2,
                     w_out_att[0].astype(BF16), w_out_ssm[0].astype(BF16), w_out[0].astype(BF16),
                     g_post[0].reshape(1, D_MODEL), tm=512)
    return out.reshape(b, seq, D_MODEL)
```

```python
import functools
import math

import numpy as np
import jax
import jax.numpy as jnp
from jax import lax
from jax.experimental import pallas as pl
from jax.experimental.pallas import tpu as pltpu

F32 = jnp.float32
BF16 = jnp.bfloat16

D_MODEL = 1024
N_META = 16
BLOCK = 128
PAD = (-N_META) % BLOCK
NORM_EPS = 1e-6

ATT_HEAD_DIM = 64
ATT_Q_HEADS = D_MODEL // ATT_HEAD_DIM
ATT_KV_HEADS = 4
ATT_GROUP = ATT_Q_HEADS // ATT_KV_HEADS
ATT_WIDTH = ATT_Q_HEADS * ATT_HEAD_DIM
KV_WIDTH = ATT_KV_HEADS * ATT_HEAD_DIM

SSM_INNER = 2 * D_MODEL
SSM_HEAD_DIM = 64
SSM_HEADS = SSM_INNER // SSM_HEAD_DIM
SSM_GROUPS = 4
SSM_HEADS_PER_GROUP = SSM_HEADS // SSM_GROUPS
SSM_GROUP_WIDTH = SSM_INNER // SSM_GROUPS
SSM_STATE = 128
CONV_WIDTH = 4
CONV_DIM = SSM_INNER + 2 * SSM_GROUPS * SSM_STATE

SPLIT_SIZES = (ATT_WIDTH, KV_WIDTH, KV_WIDTH, ATT_WIDTH, SSM_INNER, CONV_DIM, SSM_HEADS, D_MODEL, D_MODEL)

LANES = 128
SUBLANES = 8
LOG2E = math.log2(math.e)

PACK_WIDTH = ATT_WIDTH + ATT_WIDTH + SSM_INNER + 2 * D_MODEL + CONV_DIM + 2 * KV_WIDTH
Q_BLK = 0
ZATT_BLK = 1
ZSSM_BLK = 1
GATE_BLK = 2
XBC_BLK = 2
XBC_OFF = XBC_BLK * CONV_DIM
K_BLK = (PACK_WIDTH - 2 * KV_WIDTH) // KV_WIDTH
V_BLK = K_BLK + 1
PROJ_CHUNK = 512
VMEM_LIMIT = 48 * 1024 * 1024
VMEM_LIMIT_IN_PROJ = 58 * 1024 * 1024


def _sigmoid(x):
    return 0.5 * jnp.tanh(0.5 * x) + 0.5


def _silu(x):
    half = 0.5 * x
    return half * jnp.tanh(half) + half


def _const_spec(shape, n_grid):
    zeros = (0,) * len(shape)
    index_map = (lambda i: zeros) if n_grid == 1 else (lambda i, j: zeros)
    return pl.BlockSpec(shape, index_map, pipeline_mode=pl.Buffered(1))


def _inproj_kernel(h_ref, g_ref, w_ref, wdt_ref, convw_ref, convb_ref, tail0_ref, *rest,
                   tiles_per_seq, first_valid_row, emit_tail):
    if emit_tail:
        proj_ref, dt_ref, tailout_ref, tail_scr = rest
    else:
        proj_ref, dt_ref, tail_scr = rest
    tm = h_ref.shape[0]
    i = pl.program_id(0)

    @pl.when(i % tiles_per_seq == 0)
    def _():
        tail_scr[...] = tail0_ref[...]

    h = h_ref[...]
    ms = jnp.mean(h * h, axis=-1, keepdims=True)
    u = (h * lax.rsqrt(ms + NORM_EPS) * g_ref[...]).astype(BF16)
    dt_ref[...] = jnp.dot(u, wdt_ref[...], preferred_element_type=F32)

    def project(c0):
        return jnp.dot(u, w_ref[:, c0:c0 + PROJ_CHUNK], preferred_element_type=F32)

    def plain_chunk(c0):
        r = project(c0)
        if c0 < ATT_WIDTH:
            r = r * (ATT_HEAD_DIM ** -0.5)
        proj_ref[:, c0:c0 + PROJ_CHUNK] = r.astype(BF16)

    row8 = lax.broadcasted_iota(jnp.int32, (SUBLANES, 1), 0)
    if first_valid_row:
        valid = lax.broadcasted_iota(jnp.int32, (tm, 1), 0) >= first_valid_row

    def conv_chunk(c0):
        cols = slice(c0, c0 + PROJ_CHUNK)
        r = project(XBC_OFF + c0)
        tail = tail_scr[:, cols]
        acc = convb_ref[:, cols] + convw_ref[CONV_WIDTH - 1:CONV_WIDTH, cols] * r
        for s in range(1, CONV_WIDTH):
            rolled = pltpu.roll(r, s, axis=0)
            top = jnp.where(row8 < s, pltpu.roll(tail, s, axis=0), rolled[:SUBLANES])
            shifted = jnp.concatenate([top, rolled[SUBLANES:]], axis=0)
            acc = acc + convw_ref[CONV_WIDTH - 1 - s:CONV_WIDTH - s, cols] * shifted
        new_tail = r[tm - SUBLANES:]
        tail_scr[:, cols] = new_tail
        if emit_tail:
            tailout_ref[:, cols] = new_tail
        xact = _silu(acc)
        if first_valid_row:
            xact = jnp.where(valid, xact, 0.0)
        proj_ref[:, XBC_OFF + c0:XBC_OFF + c0 + PROJ_CHUNK] = xact.astype(BF16)

    for c0 in range(0, PACK_WIDTH, PROJ_CHUNK):
        if XBC_OFF <= c0 < XBC_OFF + CONV_DIM:
            conv_chunk(c0 - XBC_OFF)
        else:
            plain_chunk(c0)


def _inproj(h2, g_pre, w_pack, w_dt, conv_w, conv_b, tail0, *, tm, tiles_per_seq, first_valid_row, emit_tail):
    rows = h2.shape[0]
    out_specs = [pl.BlockSpec((tm, PACK_WIDTH), lambda i: (i, 0)), pl.BlockSpec((tm, LANES), lambda i: (i, 0))]
    out_shape = [jax.ShapeDtypeStruct((rows, PACK_WIDTH), BF16), jax.ShapeDtypeStruct((rows, LANES), F32)]
    if emit_tail:
        assert rows == tm
        out_specs.append(pl.BlockSpec((SUBLANES, CONV_DIM), lambda i: (0, 0)))
        out_shape.append(jax.ShapeDtypeStruct((SUBLANES, CONV_DIM), F32))
    return pl.pallas_call(
        functools.partial(_inproj_kernel, tiles_per_seq=tiles_per_seq, first_valid_row=first_valid_row,
                          emit_tail=emit_tail),
        grid=(rows // tm,),
        in_specs=[
            pl.BlockSpec((tm, D_MODEL), lambda i: (i, 0)),
            _const_spec((1, D_MODEL), 1),
            _const_spec((D_MODEL, PACK_WIDTH), 1),
            _const_spec((D_MODEL, LANES), 1),
            _const_spec((CONV_WIDTH, CONV_DIM), 1),
            _const_spec((1, CONV_DIM), 1),
            _const_spec((SUBLANES, CONV_DIM), 1),
        ],
        out_specs=out_specs,
        out_shape=out_shape,
        scratch_shapes=[pltpu.VMEM((SUBLANES, CONV_DIM), F32)],
        compiler_params=pltpu.CompilerParams(dimension_semantics=("arbitrary",),
                                             vmem_limit_bytes=VMEM_LIMIT_IN_PROJ),
        name="in_proj",
    )(h2, g_pre, w_pack, w_dt, conv_w, conv_b, tail0)


def _attn_kernel(sink_ref, q_ref, kc_ref, vc_ref, kp_ref, vp_ref, km_ref, vmt_ref, o_ref, *, blocks_per_step):
    j = pl.program_id(1)
    n_lanes = ATT_GROUP * BLOCK
    key = lax.broadcasted_iota(jnp.int32, (BLOCK, n_lanes), 0)
    qry = lax.broadcasted_iota(jnp.int32, (BLOCK, n_lanes), 1) % BLOCK
    in_cur = key <= qry
    rel = jnp.where(in_cur, qry - key, qry - key + BLOCK).astype(F32)
    rel_first = jnp.where(jnp.logical_or(in_cur, j > 0), rel, jnp.inf)
    cur_b = in_cur.astype(BF16)
    prev_b = 1.0 - cur_b
    low_lanes = lax.broadcasted_iota(jnp.int32, (1, LANES), 1) < ATT_HEAD_DIM
    low_rows = lax.broadcasted_iota(jnp.int32, (LANES, 1), 0) < ATT_HEAD_DIM
    nt = (((1,), (1,)), ((), ()))

    def arrange_kv(k_blk, v_blk):
        k_f, v_f = k_blk.astype(F32), v_blk.astype(F32)
        k_dup, vt_dup = [], []
        for t in range(KV_WIDTH // LANES):
            k_tile = k_f[:, t * LANES:(t + 1) * LANES]
            k_swap = pltpu.roll(k_tile, ATT_HEAD_DIM, axis=1)
            k_dup += [jnp.where(low_lanes, k_tile, k_swap).astype(BF16),
                      jnp.where(low_lanes, k_swap, k_tile).astype(BF16)]
            vt = v_f[:, t * LANES:(t + 1) * LANES].T
            vt_dup += [jnp.concatenate([vt[:ATT_HEAD_DIM]] * 2, axis=0).astype(BF16),
                       jnp.concatenate([vt[ATT_HEAD_DIM:]] * 2, axis=0).astype(BF16)]
        return k_dup, vt_dup

    prev_kv = arrange_kv(kp_ref[0], vp_ref[0])
    for blk in range(blocks_per_step):
        rows = slice(blk * BLOCK, (blk + 1) * BLOCK)
        cur_kv = arrange_kv(kc_ref[0, rows, :], vc_ref[0, rows, :])
        rel_blk = rel_first if blk == 0 else rel
        q = q_ref[0, rows, :]
        for g in range(ATT_KV_HEADS):
            k_dup = jnp.concatenate([prev_kv[0][g], cur_kv[0][g]], axis=0)
            vt_dup = jnp.concatenate([prev_kv[1][g], cur_kv[1][g]], axis=1)
            q_rows = []
            for t in (2 * g, 2 * g + 1):
                q_tile = q[:, t * LANES:(t + 1) * LANES]
                q_rows += [jnp.where(low_lanes, q_tile, 0), jnp.where(low_lanes, 0, q_tile)]
            q_cat = jnp.concatenate(q_rows, axis=0)
            s_band = lax.dot_general(k_dup, q_cat, nt, preferred_element_type=F32)
            s_meta = lax.dot_general(km_ref[g], q_cat, nt, preferred_element_type=F32)

            heads = range(g * ATT_GROUP, (g + 1) * ATT_GROUP)
            slopes = jnp.concatenate(
                [jnp.full((1, BLOCK), 2.0 ** (-8.0 * (h + 1) / ATT_Q_HEADS), F32) for h in heads], axis=1)
            sinks = jnp.concatenate([jnp.full((1, BLOCK), sink_ref[h], F32) for h in heads], axis=1)
            s = jnp.where(in_cur, s_band[BLOCK:], s_band[:BLOCK]) - slopes * rel_blk
            m = jnp.maximum(jnp.max(s, axis=0, keepdims=True), jnp.max(s_meta, axis=0, keepdims=True))
            m = jnp.maximum(m, sinks)
            p = jnp.exp(s - m)
            p_meta = jnp.exp(s_meta - m)
            denom = (jnp.sum(p, axis=0, keepdims=True) + jnp.sum(p_meta, axis=0, keepdims=True)
                     + jnp.exp(sinks - m))
            p_b = p.astype(BF16)
            p_band = jnp.concatenate([p_b * prev_b, p_b * cur_b], axis=0)
            o_t = jnp.dot(vt_dup, p_band, preferred_element_type=F32)
            o_t = o_t + jnp.dot(vmt_ref[g], p_meta.astype(BF16), preferred_element_type=F32)
            o_t = o_t * (1.0 / denom)
            for pair in range(ATT_GROUP // 2):
                head_a = o_t[:, (2 * pair) * BLOCK:(2 * pair + 1) * BLOCK]
                head_b = o_t[:, (2 * pair + 1) * BLOCK:(2 * pair + 2) * BLOCK]
                tile = 2 * g + pair
                o_ref[0, rows, tile * LANES:(tile + 1) * LANES] = jnp.where(low_rows, head_a, head_b).T.astype(BF16)
        prev_kv = cur_kv


def _attention(proj3, k_meta_dup, v_meta_t_dup, sinks, *, blocks_per_step):
    b, seq, _ = proj3.shape
    rows = blocks_per_step * BLOCK
    return pl.pallas_call(
        functools.partial(_attn_kernel, blocks_per_step=blocks_per_step),
        grid=(b, seq // rows),
        in_specs=[
            pl.BlockSpec(memory_space=pltpu.SMEM),
            pl.BlockSpec((1, rows, ATT_WIDTH), lambda i, j: (i, j, Q_BLK)),
            pl.BlockSpec((1, rows, KV_WIDTH), lambda i, j: (i, j, K_BLK)),
            pl.BlockSpec((1, rows, KV_WIDTH), lambda i, j: (i, j, V_BLK)),
            pl.BlockSpec((1, BLOCK, KV_WIDTH), lambda i, j: (i, jnp.maximum(j * blocks_per_step - 1, 0), K_BLK)),
            pl.BlockSpec((1, BLOCK, KV_WIDTH), lambda i, j: (i, jnp.maximum(j * blocks_per_step - 1, 0), V_BLK)),
            _const_spec((ATT_KV_HEADS, N_META, LANES), 2),
            _const_spec((ATT_KV_HEADS, LANES, N_META), 2),
        ],
        out_specs=pl.BlockSpec((1, rows, ATT_WIDTH), lambda i, j: (i, j, 0)),
        out_shape=jax.ShapeDtypeStruct((b, seq, ATT_WIDTH), BF16),
        compiler_params=pltpu.CompilerParams(
            dimension_semantics=("arbitrary", "arbitrary"), vmem_limit_bytes=VMEM_LIMIT),
        name="swa_attention",
    )(sinks, proj3, proj3, proj3, proj3, proj3, k_meta_dup, v_meta_t_dup)


def _meta_kv(proj_meta):
    k_off, v_off = K_BLK * KV_WIDTH, V_BLK * KV_WIDTH
    k_dup, vt_dup = [], []
    for g in range(ATT_KV_HEADS):
        kg = proj_meta[PAD:, k_off + g * ATT_HEAD_DIM:k_off + (g + 1) * ATT_HEAD_DIM]
        vg = proj_meta[PAD:, v_off + g * ATT_HEAD_DIM:v_off + (g + 1) * ATT_HEAD_DIM]
        k_dup.append(jnp.concatenate([kg, kg], axis=1))
        vt_dup.append(jnp.concatenate([vg.T, vg.T], axis=0))
    return jnp.stack(k_dup), jnp.stack(vt_dup)


def _split3(a):
    hi = a.astype(BF16).astype(F32)
    r1 = a - hi
    mid = r1.astype(BF16).astype(F32)
    return hi, mid, r1 - mid


def _expand_lhs(x):
    hi, mid, lo = _split3(x)
    return (hi + pltpu.roll(mid, SSM_HEADS, axis=1) + pltpu.roll(lo, 2 * SSM_HEADS, axis=1)).astype(BF16)


def _expand_matrix():
    r = np.arange(LANES)[:, None]
    c = np.arange(SSM_INNER)[None, :]
    return jnp.asarray((r < 3 * SSM_HEADS) & (r % SSM_HEADS == c // SSM_HEAD_DIM), BF16)


def _ssd_kernel(xact_ref, dt_ref, dtb_ref, alog_ref, dskip_ref, expand_ref, s0_ref, *rest,
                chunks_per_step, emit_y, emit_state):
    rest = list(rest)
    y_ref = rest.pop(0) if emit_y else None
    sout_ref = rest.pop(0) if emit_state else None
    state_scr, q2row_scr = rest
    c = pl.program_id(1)

    @pl.when(c == 0)
    def _():
        state_scr[...] = s0_ref[...]

    row = lax.broadcasted_iota(jnp.int32, (BLOCK, BLOCK), 0)
    col = lax.broadcasted_iota(jnp.int32, (BLOCK, BLOCK), 1)
    tril = row >= col
    tril_b = tril.astype(BF16)
    lane = lax.broadcasted_iota(jnp.int32, (1, LANES), 1)
    head_lane = lane < SSM_HEADS
    low_b = (lane < SSM_HEAD_DIM).astype(BF16)
    high_b = 1.0 - low_b
    neg_a2 = -jnp.exp(alog_ref[...]) * LOG2E
    nt = (((1,), (1,)), ((), ()))
    n_bc = SSM_GROUPS * SSM_STATE

    for ch in range(chunks_per_step):
        rows = slice(ch * BLOCK, (ch + 1) * BLOCK)
        xb = xact_ref[0, rows, :]
        x_dt = dt_ref[0, rows, :] + dtb_ref[...]
        dt = jnp.maximum(x_dt, 0.0) + jnp.log1p(jnp.exp(-jnp.abs(x_dt)))
        a2 = dt * neg_a2
        cs2 = sum(jnp.dot(tril_b, part.astype(BF16), preferred_element_type=F32) for part in _split3(a2))
        q2 = cs2 - jnp.log(dt) * LOG2E
        cs2_last = cs2[BLOCK - 1:BLOCK, :]
        q2row_scr[...] = q2.T
        w = jnp.where(head_lane, jnp.exp2(cs2_last - q2), 0.0)
        cd = jnp.where(head_lane, jnp.exp2(cs2_last), 0.0)
        parts = [_expand_lhs(w), _expand_lhs(jnp.broadcast_to(cd, (2 * SUBLANES, LANES)))]
        if emit_y:
            parts.insert(0, _expand_lhs(jnp.where(head_lane, jnp.exp2(cs2), 0.0)))
        ex = jnp.dot(jnp.concatenate(parts, axis=0), expand_ref[...], preferred_element_type=F32)
        off = BLOCK if emit_y else 0
        e_exp = ex[:BLOCK] if emit_y else None
        w_exp = ex[off:off + BLOCK]
        cd_exp = ex[off + BLOCK:off + BLOCK + 1]
        xs_f = xb[:, :SSM_INNER].astype(F32)
        xw_b = (xs_f * w_exp).astype(BF16)

        for g in range(SSM_GROUPS):
            bg = xb[:, SSM_INNER + g * SSM_STATE:SSM_INNER + (g + 1) * SSM_STATE]
            cg = xb[:, SSM_INNER + n_bc + g * SSM_STATE:SSM_INNER + n_bc + (g + 1) * SSM_STATE]
            gcols = slice(g * SSM_GROUP_WIDTH, (g + 1) * SSM_GROUP_WIDTH)
            s_prev = state_scr[:, gcols]
            bg_t = bg.astype(F32).T.astype(BF16)
            new_states = jnp.dot(bg_t, xw_b[:, gcols], preferred_element_type=F32)
            state_scr[:, gcols] = s_prev * cd_exp[:, gcols] + new_states
            if not emit_y:
                continue
            cb = lax.dot_general(cg, bg, nt, preferred_element_type=F32)
            y_off = jnp.dot(cg, s_prev.astype(BF16), preferred_element_type=F32)
            for pair in range(SSM_HEADS_PER_GROUP // 2):
                h0 = g * SSM_HEADS_PER_GROUP + 2 * pair
                cols = slice(h0 * SSM_HEAD_DIM, (h0 + 2) * SSM_HEAD_DIM)
                pcols = slice(2 * pair * SSM_HEAD_DIM, (2 * pair + 2) * SSM_HEAD_DIM)
                m_parts = []
                for h in (h0, h0 + 1):
                    expo = jnp.where(tril, cs2[:, h:h + 1] - q2row_scr[h:h + 1, :], -jnp.inf)
                    m_parts.append((cb * jnp.exp2(expo)).astype(BF16))
                x_pair = xb[:, cols]
                x_bd = jnp.concatenate([x_pair * low_b, x_pair * high_b], axis=0)
                y_pair = jnp.dot(jnp.concatenate(m_parts, axis=1), x_bd, preferred_element_type=F32)
                y_pair = y_pair + e_exp[:, cols] * y_off[:, pcols] + dskip_ref[:, cols] * xs_f[:, cols]
                y_ref[0, rows, cols] = y_pair.astype(BF16)

    if emit_state:
        @pl.when(c == pl.num_programs(1) - 1)
        def _():
            sout_ref[...] = state_scr[...]


def _ssd(proj3, dt3, dt_bias, a_log, d_skip, s0, *, chunks_per_step, emit_y, emit_state):
    b, seq, _ = proj3.shape
    rows = chunks_per_step * BLOCK
    out_specs, out_shape = [], []
    if emit_y:
        out_specs.append(pl.BlockSpec((1, rows, SSM_INNER), lambda i, j: (i, j, 0)))
        out_shape.append(jax.ShapeDtypeStruct((b, seq, SSM_INNER), BF16))
    if emit_state:
        assert b == 1
        out_specs.append(pl.BlockSpec((SSM_STATE, SSM_INNER), lambda i, j: (0, 0)))
        out_shape.append(jax.ShapeDtypeStruct((SSM_STATE, SSM_INNER), F32))
    return pl.pallas_call(
        functools.partial(_ssd_kernel, chunks_per_step=chunks_per_step, emit_y=emit_y, emit_state=emit_state),
        grid=(b, seq // rows),
        in_specs=[
            pl.BlockSpec((1, rows, CONV_DIM), lambda i, j: (i, j, XBC_BLK)),
            pl.BlockSpec((1, rows, LANES), lambda i, j: (i, j, 0)),
            _const_spec((1, LANES), 2),
            _const_spec((1, LANES), 2),
            _const_spec((1, SSM_INNER), 2),
            _const_spec((LANES, SSM_INNER), 2),
            _const_spec((SSM_STATE, SSM_INNER), 2),
        ],
        out_specs=out_specs,
        out_shape=out_shape,
        scratch_shapes=[
            pltpu.VMEM((SSM_STATE, SSM_INNER), F32),
            pltpu.VMEM((LANES, BLOCK), F32),
        ],
        compiler_params=pltpu.CompilerParams(
            dimension_semantics=("arbitrary", "arbitrary"), vmem_limit_bytes=VMEM_LIMIT),
        name="ssd",
    )(proj3, dt3, dt_bias, a_log, d_skip, _expand_matrix(), s0)


def _out_kernel(o_ref, y_ref, zatt_ref, zssm_ref, gate_ref, x_ref, gnorm_ref, woa_ref, wos_ref, wo_ref,
                gpost_ref, out_ref):
    att_in = (o_ref[...].astype(F32) * _silu(zatt_ref[...].astype(F32))).astype(BF16)
    y = y_ref[...].astype(F32) * _silu(zssm_ref[...].astype(F32))
    normed = []
    for g in range(SSM_GROUPS):
        yg = y[:, g * SSM_GROUP_WIDTH:(g + 1) * SSM_GROUP_WIDTH]
        normed.append(yg * lax.rsqrt(jnp.mean(yg * yg, axis=-1, keepdims=True) + NORM_EPS))
    ssm_in = (jnp.concatenate(normed, axis=1) * gnorm_ref[...]).astype(BF16)
    ya = jnp.dot(att_in, woa_ref[...], preferred_element_type=F32)
    ys = jnp.dot(ssm_in, wos_ref[...], preferred_element_type=F32)
    gates = _sigmoid(gate_ref[...].astype(F32))
    merged = gates[:, :D_MODEL] * ya + gates[:, D_MODEL:] * ys
    out = jnp.dot(merged.astype(BF16), wo_ref[...], preferred_element_type=F32)
    ms = jnp.mean(out * out, axis=-1, keepdims=True)
    out_ref[...] = x_ref[...] + out * lax.rsqrt(ms + NORM_EPS) * gpost_ref[...]


def _merge_out(o_att, y_ssm, proj, x2, g_norm, w_out_att, w_out_ssm, w_out, g_post, *, tm):
    rows = x2.shape[0]
    return pl.pallas_call(
        _out_kernel,
        grid=(rows // tm,),
        in_specs=[
            pl.BlockSpec((tm, ATT_WIDTH), lambda i: (i, 0)),
            pl.BlockSpec((tm, SSM_INNER), lambda i: (i, 0)),
            pl.BlockSpec((tm, ATT_WIDTH), lambda i: (i, ZATT_BLK)),
            pl.BlockSpec((tm, SSM_INNER), lambda i: (i, ZSSM_BLK)),
            pl.BlockSpec((tm, 2 * D_MODEL), lambda i: (i, GATE_BLK)),
            pl.BlockSpec((tm, D_MODEL), lambda i: (i, 0)),
            _const_spec((1, SSM_INNER), 1),
            _const_spec((ATT_WIDTH, D_MODEL), 1),
            _const_spec((SSM_INNER, D_MODEL), 1),
            _const_spec((D_MODEL, D_MODEL), 1),
            _const_spec((1, D_MODEL), 1),
        ],
        out_specs=pl.BlockSpec((tm, D_MODEL), lambda i: (i, 0)),
        out_shape=jax.ShapeDtypeStruct((rows, D_MODEL), F32),
        compiler_params=pltpu.CompilerParams(dimension_semantics=("arbitrary",), vmem_limit_bytes=VMEM_LIMIT),
        name="merge_out",
    )(o_att, y_ssm, proj, proj, proj, x2, g_norm, w_out_att, w_out_ssm, w_out, g_post)


def _split_cols(w):
    parts, start = [], 0
    for size in SPLIT_SIZES:
        parts.append(w[:, start:start + size])
        start += size
    return parts


def _pack_in_proj(w_in):
    q, k, v, z_att, z_ssm, xbc, dt, gate_att, gate_ssm = _split_cols(w_in)
    w_pack = jnp.concatenate([q, z_att, z_ssm, gate_att, gate_ssm, xbc, k, v], axis=1).astype(BF16)
    w_dt = jnp.pad(dt, ((0, 0), (0, LANES - SSM_HEADS))).astype(BF16)
    return w_pack, w_dt


def _pad_lanes(v):
    return jnp.pad(v.astype(F32), (0, LANES - v.shape[0])).reshape(1, LANES)


def kernel(x, meta_tokens, g_pre, w_in, conv_w, conv_b, dt_bias, a_log, d_skip, attn_sinks, g_ssm_norm,
           w_out_att, w_out_ssm, w_out, g_post):
    b, seq, d = x.shape
    assert d == D_MODEL and PAD + N_META == BLOCK
    assert g_pre.shape[0] == 1, "one layer"
    in_tile, out_tile, blocks_per_step = 512, 512, 4
    assert seq % in_tile == 0 and seq % out_tile == 0 and seq % (blocks_per_step * BLOCK) == 0

    w_pack, w_dt = _pack_in_proj(w_in[0])
    g_pre2 = g_pre[0].reshape(1, D_MODEL)
    conv_b2 = conv_b[0].reshape(1, CONV_DIM)
    ssd_params = (_pad_lanes(dt_bias[0]), _pad_lanes(a_log[0]),
                  jnp.repeat(d_skip[0].astype(F32), SSM_HEAD_DIM).reshape(1, SSM_INNER))

    h_meta = jnp.concatenate([jnp.zeros((PAD, D_MODEL), x.dtype), meta_tokens.astype(x.dtype)], axis=0)
    proj_m, dt_m, tail_m = _inproj(h_meta, g_pre2, w_pack, w_dt, conv_w[0], conv_b2,
                                   jnp.zeros((SUBLANES, CONV_DIM), F32),
                                   tm=BLOCK, tiles_per_seq=1, first_valid_row=PAD, emit_tail=True)
    (state_m,) = _ssd(proj_m.reshape(1, BLOCK, PACK_WIDTH), dt_m.reshape(1, BLOCK, LANES), *ssd_params,
                      jnp.zeros((SSM_STATE, SSM_INNER), F32), chunks_per_step=1, emit_y=False, emit_state=True)

    x2 = x.reshape(b * seq, D_MODEL)
    proj, dt = _inproj(x2, g_pre2, w_pack, w_dt, conv_w[0], conv_b2, tail_m,
                       tm=in_tile, tiles_per_seq=seq // in_tile, first_valid_row=0, emit_tail=False)
    proj3 = proj.reshape(b, seq, PACK_WIDTH)
    o_att = _attention(proj3, *_meta_kv(proj_m), attn_sinks[0].astype(F32), blocks_per_step=blocks_per_step)
    (y_ssm,) = _ssd(proj3, dt.reshape(b, seq, LANES), *ssd_params, state_m,
                    chunks_per_step=blocks_per_step, emit_y=True, emit_state=False)
    out = _merge_out(o_att.reshape(b * seq, ATT_WIDTH), y_ssm.reshape(b * seq, SSM_INNER), proj, x2,
                     g_ssm_norm[0].reshape(1, SSM_INNER), w_out_att[0].astype(BF16), w_out_ssm[0].astype(BF16),
                     w_out[0].astype(BF16), g_post[0].reshape(1, D_MODEL), tm=out_tile)
    return out.reshape(b, seq, D_MODEL)
```

```python
import functools
import math

import numpy as np
import jax
import jax.numpy as jnp
from jax import lax
from jax.experimental import pallas as pl
from jax.experimental.pallas import tpu as pltpu

F32 = jnp.float32
BF16 = jnp.bfloat16

D_MODEL = 1024
N_META = 16
BLOCK = 128
PAD = (-N_META) % BLOCK
NORM_EPS = 1e-6

ATT_HEAD_DIM = 64
ATT_Q_HEADS = D_MODEL // ATT_HEAD_DIM
ATT_KV_HEADS = 4
ATT_GROUP = ATT_Q_HEADS // ATT_KV_HEADS
ATT_WIDTH = ATT_Q_HEADS * ATT_HEAD_DIM
KV_WIDTH = ATT_KV_HEADS * ATT_HEAD_DIM

SSM_INNER = 2 * D_MODEL
SSM_HEAD_DIM = 64
SSM_HEADS = SSM_INNER // SSM_HEAD_DIM
SSM_GROUPS = 4
SSM_HEADS_PER_GROUP = SSM_HEADS // SSM_GROUPS
SSM_GROUP_WIDTH = SSM_INNER // SSM_GROUPS
SSM_STATE = 128
CONV_WIDTH = 4
CONV_DIM = SSM_INNER + 2 * SSM_GROUPS * SSM_STATE

SPLIT_SIZES = (ATT_WIDTH, KV_WIDTH, KV_WIDTH, ATT_WIDTH, SSM_INNER, CONV_DIM, SSM_HEADS, D_MODEL, D_MODEL)

LANES = 128
SUBLANES = 8
LOG2E = math.log2(math.e)

PACK_WIDTH = ATT_WIDTH + ATT_WIDTH + SSM_INNER + 2 * D_MODEL + CONV_DIM + 2 * KV_WIDTH
Q_BLK = 0
ZATT_BLK = 1
ZSSM_BLK = 1
GATE_BLK = 2
XBC_BLK = 2
XBC_OFF = XBC_BLK * CONV_DIM
K_BLK = (PACK_WIDTH - 2 * KV_WIDTH) // KV_WIDTH
V_BLK = K_BLK + 1
PROJ_CHUNK = 256
CONV_ROWS = 64
W_Q_OFF = 0
W_KV_OFF = ATT_WIDTH
W_ZATT_OFF = W_KV_OFF + 2 * KV_WIDTH
W_ZSSM_OFF = W_ZATT_OFF + ATT_WIDTH
W_XBC_OFF = W_ZSSM_OFF + SSM_INNER
W_MAIN_WIDTH = W_XBC_OFF + CONV_DIM
W_DT_OFF = W_MAIN_WIDTH
W_GATE_OFF = W_DT_OFF + SSM_HEADS
VMEM_LIMIT = 48 * 1024 * 1024
VMEM_LIMIT_IN_PROJ = 58 * 1024 * 1024


def _sigmoid(x):
    return 0.5 * jnp.tanh(0.5 * x) + 0.5


def _silu(x):
    half = 0.5 * x
    return half * jnp.tanh(half) + half


def _const_spec(shape, n_grid):
    zeros = (0,) * len(shape)
    index_map = (lambda i: zeros) if n_grid == 1 else (lambda i, j: zeros)
    return pl.BlockSpec(shape, index_map, pipeline_mode=pl.Buffered(1))


def _inproj_kernel(h_ref, g_ref, w_ref, wgate_ref, wdt_ref, convw_ref, convb_ref, tail0_ref, *rest,
                   tiles_per_seq, first_valid_row, emit_tail, with_attention):
    rest = list(rest)
    if with_attention:
        sink_ref, km_ref, vmt_ref = rest[:3]
        del rest[:3]
    proj_ref, dt_ref = rest[:2]
    del rest[:2]
    tailout_ref = rest.pop(0) if emit_tail else None
    o_ref = rest.pop(0) if with_attention else None
    tail_scr, raw_scr = rest[:2]
    if with_attention:
        kd_scr, vtd_scr = rest[2:]
    tm = h_ref.shape[0]
    i = pl.program_id(0)
    first_tile = i % tiles_per_seq == 0

    @pl.when(first_tile)
    def _():
        tail_scr[...] = tail0_ref[...]

    h = h_ref[...]
    ms = jnp.mean(h * h, axis=-1, keepdims=True)
    u = (h * lax.rsqrt(ms + NORM_EPS) * g_ref[...]).astype(BF16)
    dt_ref[...] = jnp.dot(u, wdt_ref[...], preferred_element_type=F32)

    def project(src_ref, c0):
        return jnp.dot(u, src_ref[:, c0:c0 + PROJ_CHUNK], preferred_element_type=F32)

    def plain_chunk(src_ref, src_c0, out_c0):
        r = project(src_ref, src_c0)
        if out_c0 < ZATT_BLK * ATT_WIDTH:
            r = r * (ATT_HEAD_DIM ** -0.5)
        elif out_c0 < GATE_BLK * 2 * D_MODEL:
            r = _silu(r)
        elif out_c0 < XBC_OFF:
            r = _sigmoid(r)
        proj_ref[:, out_c0:out_c0 + PROJ_CHUNK] = r.astype(BF16)

    def xbc_matmul(k):
        cols = slice(k * PROJ_CHUNK, (k + 1) * PROJ_CHUNK)
        raw = raw_scr.at[k % 2]
        raw[0:SUBLANES, :] = tail_scr[:, cols]
        raw[SUBLANES:, :] = project(w_ref, W_XBC_OFF + k * PROJ_CHUNK)
        new_tail = raw[tm:, :]
        tail_scr[:, cols] = new_tail
        if emit_tail:
            tailout_ref[:, cols] = new_tail

    def conv_piece(k, r0):
        raw = raw_scr.at[k % 2]
        for l0 in range(0, PROJ_CHUNK, LANES):
            cols = slice(k * PROJ_CHUNK + l0, k * PROJ_CHUNK + l0 + LANES)
            w0, w1, w2, w3 = (convw_ref[t:t + 1, cols] for t in range(CONV_WIDTH))
            ext = raw[r0:r0 + SUBLANES + CONV_ROWS, l0:l0 + LANES]
            ext1 = pltpu.roll(ext, 1, axis=0)
            pair_old = w1 * ext + w0 * ext1
            acc = (convb_ref[:, cols] + w3 * ext[SUBLANES:] + w2 * ext1[SUBLANES:]
                   + pltpu.roll(pair_old, 2, axis=0)[SUBLANES:])
            xact = _silu(acc)
            if first_valid_row:
                rows = lax.broadcasted_iota(jnp.int32, (CONV_ROWS, 1), 0) + r0
                xact = jnp.where(rows >= first_valid_row, xact, 0.0)
            proj_ref[r0:r0 + CONV_ROWS, XBC_OFF + cols.start:XBC_OFF + cols.stop] = xact.astype(BF16)

    if with_attention:
        n_blocks = tm // BLOCK
        n_lanes = ATT_GROUP * BLOCK
        key = lax.broadcasted_iota(jnp.int32, (BLOCK, n_lanes), 0)
        qry = lax.broadcasted_iota(jnp.int32, (BLOCK, n_lanes), 1) % BLOCK
        in_cur = key <= qry
        rel = jnp.where(in_cur, qry - key, qry - key + BLOCK).astype(F32)
        rel_first = jnp.where(jnp.logical_or(in_cur, jnp.logical_not(first_tile)), rel, jnp.inf)
        cur_b = in_cur.astype(BF16)
        prev_b = 1.0 - cur_b
        low_lanes = lax.broadcasted_iota(jnp.int32, (1, LANES), 1) < ATT_HEAD_DIM
        low_rows = lax.broadcasted_iota(jnp.int32, (LANES, 1), 0) < ATT_HEAD_DIM
        nt = (((1,), (1,)), ((), ()))

        @pl.when(first_tile)
        def _():
            kd_scr[0] = jnp.zeros(kd_scr.shape[1:], BF16)
            vtd_scr[0] = jnp.zeros(vtd_scr.shape[1:], BF16)

        @pl.when(jnp.logical_not(first_tile))
        def _():
            kd_scr[0] = kd_scr[n_blocks]
            vtd_scr[0] = vtd_scr[n_blocks]

    def arrange_kv(blk):
        rows = slice(blk * BLOCK, (blk + 1) * BLOCK)
        k_f = proj_ref[rows, K_BLK * KV_WIDTH:(K_BLK + 1) * KV_WIDTH].astype(F32)
        v_f = proj_ref[rows, V_BLK * KV_WIDTH:(V_BLK + 1) * KV_WIDTH].astype(F32)
        for t in range(KV_WIDTH // LANES):
            k_tile = k_f[:, t * LANES:(t + 1) * LANES]
            k_swap = pltpu.roll(k_tile, ATT_HEAD_DIM, axis=1)
            kd_scr[blk + 1, 2 * t] = jnp.where(low_lanes, k_tile, k_swap).astype(BF16)
            kd_scr[blk + 1, 2 * t + 1] = jnp.where(low_lanes, k_swap, k_tile).astype(BF16)
            vt = v_f[:, t * LANES:(t + 1) * LANES].T
            vtd_scr[blk + 1, 2 * t] = jnp.concatenate([vt[:ATT_HEAD_DIM]] * 2, axis=0).astype(BF16)
            vtd_scr[blk + 1, 2 * t + 1] = jnp.concatenate([vt[ATT_HEAD_DIM:]] * 2, axis=0).astype(BF16)

    def attend(blk, g):
        rows = slice(blk * BLOCK, (blk + 1) * BLOCK)
        k_dup = jnp.concatenate([kd_scr[blk, g], kd_scr[blk + 1, g]], axis=0)
        vt_dup = jnp.concatenate([vtd_scr[blk, g], vtd_scr[blk + 1, g]], axis=1)
        q_rows = []
        for t in (2 * g, 2 * g + 1):
            q_tile = proj_ref[rows, Q_BLK * ATT_WIDTH + t * LANES:Q_BLK * ATT_WIDTH + (t + 1) * LANES]
            q_rows += [jnp.where(low_lanes, q_tile, 0), jnp.where(low_lanes, 0, q_tile)]
        q_cat = jnp.concatenate(q_rows, axis=0)
        s_band = lax.dot_general(k_dup, q_cat, nt, preferred_element_type=F32)
        s_meta = lax.dot_general(km_ref[g], q_cat, nt, preferred_element_type=F32)

        heads = range(g * ATT_GROUP, (g + 1) * ATT_GROUP)
        slopes = jnp.concatenate(
            [jnp.full((1, BLOCK), 2.0 ** (-8.0 * (h + 1) / ATT_Q_HEADS), F32) for h in heads], axis=1)
        sinks = jnp.concatenate([jnp.full((1, BLOCK), sink_ref[h], F32) for h in heads], axis=1)
        s = jnp.where(in_cur, s_band[BLOCK:], s_band[:BLOCK]) - slopes * (rel_first if blk == 0 else rel)
        m = jnp.maximum(jnp.max(s, axis=0, keepdims=True), jnp.max(s_meta, axis=0, keepdims=True))
        m = jnp.maximum(m, sinks)
        p = jnp.exp(s - m)
        p_meta = jnp.exp(s_meta - m)
        denom = (jnp.sum(p, axis=0, keepdims=True) + jnp.sum(p_meta, axis=0, keepdims=True)
                 + jnp.exp(sinks - m))
        p_b = p.astype(BF16)
        p_band = jnp.concatenate([p_b * prev_b, p_b * cur_b], axis=0)
        o_t = jnp.dot(vt_dup, p_band, preferred_element_type=F32)
        o_t = o_t + jnp.dot(vmt_ref[g], p_meta.astype(BF16), preferred_element_type=F32)
        o_t = o_t * (1.0 / denom)
        for pair in range(ATT_GROUP // 2):
            head_a = o_t[:, (2 * pair) * BLOCK:(2 * pair + 1) * BLOCK]
            head_b = o_t[:, (2 * pair + 1) * BLOCK:(2 * pair + 2) * BLOCK]
            tile = 2 * g + pair
            o_ref[rows, tile * LANES:(tile + 1) * LANES] = jnp.where(low_rows, head_a, head_b).T.astype(BF16)

    early, late = _plain_chunks(w_ref, wgate_ref)
    for chunk in early:
        plain_chunk(*chunk)
    attn_pending = []
    if with_attention:
        for blk in range(n_blocks):
            arrange_kv(blk)
        attn_pending = [(blk, g) for blk in range(n_blocks) for g in range(ATT_KV_HEADS)]
    n_xbc = CONV_DIM // PROJ_CHUNK
    n_items = n_xbc + len(late)
    xbc_at = {k * n_items // n_xbc: k for k in range(n_xbc)}
    late_iter = iter(late)
    matmuls = [xbc_at[n] if n in xbc_at else next(late_iter) for n in range(n_items)]
    conv_per_matmul = -(-n_xbc * (tm // CONV_ROWS) // (len(matmuls) - 1))
    attn_per_matmul = -(-len(attn_pending) // len(matmuls))
    conv_pending = []
    for item in matmuls:
        if isinstance(item, int):
            assert all(k > item - 2 for k, _ in conv_pending), "raw_scr slot still in use"
            xbc_matmul(item)
            conv_pending += [(item, r0) for r0 in range(0, tm, CONV_ROWS)]
        else:
            plain_chunk(*item)
        for piece in conv_pending[:conv_per_matmul]:
            conv_piece(*piece)
        conv_pending = conv_pending[conv_per_matmul:]
        for piece in attn_pending[:attn_per_matmul]:
            attend(*piece)
        attn_pending = attn_pending[attn_per_matmul:]
    for piece in conv_pending:
        conv_piece(*piece)
    for piece in attn_pending:
        attend(*piece)


def _plain_chunks(w_ref, wgate_ref):
    def chunks(ref, src, dst, width):
        return [(ref, src + c, dst + c) for c in range(0, width, PROJ_CHUNK)]

    early = chunks(w_ref, W_Q_OFF, Q_BLK * ATT_WIDTH, ATT_WIDTH) + chunks(w_ref, W_KV_OFF, K_BLK * KV_WIDTH, 2 * KV_WIDTH)
    late = (chunks(w_ref, W_ZATT_OFF, ZATT_BLK * ATT_WIDTH, ATT_WIDTH)
            + chunks(w_ref, W_ZSSM_OFF, ZSSM_BLK * SSM_INNER, SSM_INNER)
            + chunks(wgate_ref, 0, GATE_BLK * 2 * D_MODEL, 2 * D_MODEL))
    return early, late


def _inproj(h2, g_pre, w_main, w_gate, w_dt, conv_w, conv_b, tail0, attention=None, *, tm, tiles_per_seq,
            first_valid_row, emit_tail):
    rows = h2.shape[0]
    assert tm % CONV_ROWS == 0 and tm % BLOCK == 0
    operands = [h2, g_pre, w_main, w_gate, w_dt, conv_w, conv_b, tail0]
    in_specs = [
        pl.BlockSpec((tm, D_MODEL), lambda i: (i, 0)),
        _const_spec((1, D_MODEL), 1),
        _const_spec((D_MODEL, W_MAIN_WIDTH), 1),
        _const_spec((D_MODEL, 2 * D_MODEL), 1),
        _const_spec((D_MODEL, LANES), 1),
        _const_spec((CONV_WIDTH, CONV_DIM), 1),
        _const_spec((1, CONV_DIM), 1),
        _const_spec((SUBLANES, CONV_DIM), 1),
    ]
    out_specs = [pl.BlockSpec((tm, PACK_WIDTH), lambda i: (i, 0)), pl.BlockSpec((tm, LANES), lambda i: (i, 0))]
    out_shape = [jax.ShapeDtypeStruct((rows, PACK_WIDTH), BF16), jax.ShapeDtypeStruct((rows, LANES), F32)]
    scratch = [pltpu.VMEM((SUBLANES, CONV_DIM), F32),
               pltpu.VMEM((2, SUBLANES + tm, PROJ_CHUNK), F32)]
    if emit_tail:
        assert rows == tm
        out_specs.append(pl.BlockSpec((SUBLANES, CONV_DIM), lambda i: (0, 0)))
        out_shape.append(jax.ShapeDtypeStruct((SUBLANES, CONV_DIM), F32))
    if attention is not None:
        operands += list(attention)
        in_specs += [pl.BlockSpec(memory_space=pltpu.SMEM),
                     _const_spec((ATT_KV_HEADS, N_META, LANES), 1),
                     _const_spec((ATT_KV_HEADS, LANES, N_META), 1)]
        out_specs.append(pl.BlockSpec((tm, ATT_WIDTH), lambda i: (i, 0)))
        out_shape.append(jax.ShapeDtypeStruct((rows, ATT_WIDTH), BF16))
        kv_slots = (tm // BLOCK + 1, ATT_KV_HEADS, BLOCK, LANES)
        scratch += [pltpu.VMEM(kv_slots, BF16), pltpu.VMEM(kv_slots, BF16)]
    return pl.pallas_call(
        functools.partial(_inproj_kernel, tiles_per_seq=tiles_per_seq, first_valid_row=first_valid_row,
                          emit_tail=emit_tail, with_attention=attention is not None),
        grid=(rows // tm,),
        in_specs=in_specs,
        out_specs=out_specs,
        out_shape=out_shape,
        scratch_shapes=scratch,
        compiler_params=pltpu.CompilerParams(dimension_semantics=("arbitrary",),
                                             vmem_limit_bytes=VMEM_LIMIT_IN_PROJ),
        name="in_proj",
    )(*operands)


def _meta_kv(proj_meta):
    k_off, v_off = K_BLK * KV_WIDTH, V_BLK * KV_WIDTH
    k_dup, vt_dup = [], []
    for g in range(ATT_KV_HEADS):
        kg = proj_meta[PAD:, k_off + g * ATT_HEAD_DIM:k_off + (g + 1) * ATT_HEAD_DIM]
        vg = proj_meta[PAD:, v_off + g * ATT_HEAD_DIM:v_off + (g + 1) * ATT_HEAD_DIM]
        k_dup.append(jnp.concatenate([kg, kg], axis=1))
        vt_dup.append(jnp.concatenate([vg.T, vg.T], axis=0))
    return jnp.stack(k_dup), jnp.stack(vt_dup)


def _split3(a):
    hi = a.astype(BF16).astype(F32)
    r1 = a - hi
    mid = r1.astype(BF16).astype(F32)
    return hi, mid, r1 - mid


def _expand_lhs(x):
    hi, mid, lo = _split3(x)
    return (hi + pltpu.roll(mid, SSM_HEADS, axis=1) + pltpu.roll(lo, 2 * SSM_HEADS, axis=1)).astype(BF16)


def _expand_matrix():
    r = np.arange(LANES)[:, None]
    c = np.arange(SSM_INNER)[None, :]
    return jnp.asarray((r < 3 * SSM_HEADS) & (r % SSM_HEADS == c // SSM_HEAD_DIM), BF16)


def _ssd_kernel(xact_ref, dt_ref, dtb_ref, alog_ref, dskip_ref, expand_ref, s0_ref, *rest,
                chunks_per_step, emit_y, emit_state):
    rest = list(rest)
    y_ref = rest.pop(0) if emit_y else None
    sout_ref = rest.pop(0) if emit_state else None
    state_scr, q2row_scr = rest
    c = pl.program_id(1)

    @pl.when(c == 0)
    def _():
        state_scr[...] = s0_ref[...]

    row = lax.broadcasted_iota(jnp.int32, (BLOCK, BLOCK), 0)
    col = lax.broadcasted_iota(jnp.int32, (BLOCK, BLOCK), 1)
    tril = row >= col
    tril_b = tril.astype(BF16)
    lane = lax.broadcasted_iota(jnp.int32, (1, LANES), 1)
    head_lane = lane < SSM_HEADS
    low_b = (lane < SSM_HEAD_DIM).astype(BF16)
    high_b = 1.0 - low_b
    neg_a2 = -jnp.exp(alog_ref[...]) * LOG2E
    nt = (((1,), (1,)), ((), ()))
    n_bc = SSM_GROUPS * SSM_STATE

    for ch in range(chunks_per_step):
        rows = slice(ch * BLOCK, (ch + 1) * BLOCK)
        xb = xact_ref[0, rows, :]
        x_dt = dt_ref[0, rows, :] + dtb_ref[...]
        dt = jnp.maximum(x_dt, 0.0) + jnp.log1p(jnp.exp(-jnp.abs(x_dt)))
        a2 = dt * neg_a2
        cs2 = sum(jnp.dot(tril_b, part.astype(BF16), preferred_element_type=F32) for part in _split3(a2))
        q2 = cs2 - jnp.log(dt) * LOG2E
        cs2_last = cs2[BLOCK - 1:BLOCK, :]
        q2row_scr[...] = q2.T
        w = jnp.where(head_lane, jnp.exp2(cs2_last - q2), 0.0)
        cd = jnp.where(head_lane, jnp.exp2(cs2_last), 0.0)
        parts = [_expand_lhs(w), _expand_lhs(jnp.broadcast_to(cd, (2 * SUBLANES, LANES)))]
        if emit_y:
            parts.insert(0, _expand_lhs(jnp.where(head_lane, jnp.exp2(cs2), 0.0)))
        ex = jnp.dot(jnp.concatenate(parts, axis=0), expand_ref[...], preferred_element_type=F32)
        off = BLOCK if emit_y else 0
        e_exp = ex[:BLOCK] if emit_y else None
        w_exp = ex[off:off + BLOCK]
        cd_exp = ex[off + BLOCK:off + BLOCK + 1]
        xs_f = xb[:, :SSM_INNER].astype(F32)
        xw_b = (xs_f * w_exp).astype(BF16)

        for g in range(SSM_GROUPS):
            bg = xb[:, SSM_INNER + g * SSM_STATE:SSM_INNER + (g + 1) * SSM_STATE]
            cg = xb[:, SSM_INNER + n_bc + g * SSM_STATE:SSM_INNER + n_bc + (g + 1) * SSM_STATE]
            gcols = slice(g * SSM_GROUP_WIDTH, (g + 1) * SSM_GROUP_WIDTH)
            s_prev = state_scr[:, gcols]
            bg_t = bg.astype(F32).T.astype(BF16)
            new_states = jnp.dot(bg_t, xw_b[:, gcols], preferred_element_type=F32)
            state_scr[:, gcols] = s_prev * cd_exp[:, gcols] + new_states
            if not emit_y:
                continue
            cb = lax.dot_general(cg, bg, nt, preferred_element_type=F32)
            y_off = jnp.dot(cg, s_prev.astype(BF16), preferred_element_type=F32)
            for pair in range(SSM_HEADS_PER_GROUP // 2):
                h0 = g * SSM_HEADS_PER_GROUP + 2 * pair
                cols = slice(h0 * SSM_HEAD_DIM, (h0 + 2) * SSM_HEAD_DIM)
                pcols = slice(2 * pair * SSM_HEAD_DIM, (2 * pair + 2) * SSM_HEAD_DIM)
                m_parts = []
                for h in (h0, h0 + 1):
                    expo = jnp.where(tril, cs2[:, h:h + 1] - q2row_scr[h:h + 1, :], -jnp.inf)
                    m_parts.append((cb * jnp.exp2(expo)).astype(BF16))
                x_pair = xb[:, cols]
                x_bd = jnp.concatenate([x_pair * low_b, x_pair * high_b], axis=0)
                y_pair = jnp.dot(jnp.concatenate(m_parts, axis=1), x_bd, preferred_element_type=F32)
                y_pair = y_pair + e_exp[:, cols] * y_off[:, pcols] + dskip_ref[:, cols] * xs_f[:, cols]
                y_ref[0, rows, cols] = y_pair.astype(BF16)

    if emit_state:
        @pl.when(c == pl.num_programs(1) - 1)
        def _():
            sout_ref[...] = state_scr[...]


def _ssd(proj3, dt3, dt_bias, a_log, d_skip, s0, *, chunks_per_step, emit_y, emit_state):
    b, seq, _ = proj3.shape
    rows = chunks_per_step * BLOCK
    out_specs, out_shape = [], []
    if emit_y:
        out_specs.append(pl.BlockSpec((1, rows, SSM_INNER), lambda i, j: (i, j, 0)))
        out_shape.append(jax.ShapeDtypeStruct((b, seq, SSM_INNER), BF16))
    if emit_state:
        assert b == 1
        out_specs.append(pl.BlockSpec((SSM_STATE, SSM_INNER), lambda i, j: (0, 0)))
        out_shape.append(jax.ShapeDtypeStruct((SSM_STATE, SSM_INNER), F32))
    return pl.pallas_call(
        functools.partial(_ssd_kernel, chunks_per_step=chunks_per_step, emit_y=emit_y, emit_state=emit_state),
        grid=(b, seq // rows),
        in_specs=[
            pl.BlockSpec((1, rows, CONV_DIM), lambda i, j: (i, j, XBC_BLK)),
            pl.BlockSpec((1, rows, LANES), lambda i, j: (i, j, 0)),
            _const_spec((1, LANES), 2),
            _const_spec((1, LANES), 2),
            _const_spec((1, SSM_INNER), 2),
            _const_spec((LANES, SSM_INNER), 2),
            _const_spec((SSM_STATE, SSM_INNER), 2),
        ],
        out_specs=out_specs,
        out_shape=out_shape,
        scratch_shapes=[
            pltpu.VMEM((SSM_STATE, SSM_INNER), F32),
            pltpu.VMEM((LANES, BLOCK), F32),
        ],
        compiler_params=pltpu.CompilerParams(
            dimension_semantics=("arbitrary", "arbitrary"), vmem_limit_bytes=VMEM_LIMIT),
        name="ssd",
    )(proj3, dt3, dt_bias, a_log, d_skip, _expand_matrix(), s0)


def _out_kernel(o_ref, y_ref, zatt_ref, zssm_ref, gate_ref, x_ref, gnorm_ref, woa_ref, wos_ref, wo_ref,
                gpost_ref, out_ref):
    att_in = (o_ref[...].astype(F32) * zatt_ref[...].astype(F32)).astype(BF16)
    y = y_ref[...].astype(F32) * zssm_ref[...].astype(F32)
    normed = []
    for g in range(SSM_GROUPS):
        yg = y[:, g * SSM_GROUP_WIDTH:(g + 1) * SSM_GROUP_WIDTH]
        normed.append(yg * lax.rsqrt(jnp.mean(yg * yg, axis=-1, keepdims=True) + NORM_EPS))
    ssm_in = (jnp.concatenate(normed, axis=1) * gnorm_ref[...]).astype(BF16)
    ya = jnp.dot(att_in, woa_ref[...], preferred_element_type=F32)
    ys = jnp.dot(ssm_in, wos_ref[...], preferred_element_type=F32)
    gates = gate_ref[...].astype(F32)
    merged = gates[:, :D_MODEL] * ya + gates[:, D_MODEL:] * ys
    out = jnp.dot(merged.astype(BF16), wo_ref[...], preferred_element_type=F32)
    ms = jnp.mean(out * out, axis=-1, keepdims=True)
    out_ref[...] = x_ref[...] + out * lax.rsqrt(ms + NORM_EPS) * gpost_ref[...]


def _merge_out(o_att, y_ssm, proj, x2, g_norm, w_out_att, w_out_ssm, w_out, g_post, *, tm):
    rows = x2.shape[0]
    return pl.pallas_call(
        _out_kernel,
        grid=(rows // tm,),
        in_specs=[
            pl.BlockSpec((tm, ATT_WIDTH), lambda i: (i, 0)),
            pl.BlockSpec((tm, SSM_INNER), lambda i: (i, 0)),
            pl.BlockSpec((tm, ATT_WIDTH), lambda i: (i, ZATT_BLK)),
            pl.BlockSpec((tm, SSM_INNER), lambda i: (i, ZSSM_BLK)),
            pl.BlockSpec((tm, 2 * D_MODEL), lambda i: (i, GATE_BLK)),
            pl.BlockSpec((tm, D_MODEL), lambda i: (i, 0)),
            _const_spec((1, SSM_INNER), 1),
            _const_spec((ATT_WIDTH, D_MODEL), 1),
            _const_spec((SSM_INNER, D_MODEL), 1),
            _const_spec((D_MODEL, D_MODEL), 1),
            _const_spec((1, D_MODEL), 1),
        ],
        out_specs=pl.BlockSpec((tm, D_MODEL), lambda i: (i, 0)),
        out_shape=jax.ShapeDtypeStruct((rows, D_MODEL), F32),
        compiler_params=pltpu.CompilerParams(dimension_semantics=("arbitrary",), vmem_limit_bytes=VMEM_LIMIT),
        name="merge_out",
    )(o_att, y_ssm, proj, proj, proj, x2, g_norm, w_out_att, w_out_ssm, w_out, g_post)


def _in_proj_weights(w_in):
    assert w_in.shape[1] == W_GATE_OFF + 2 * D_MODEL == sum(SPLIT_SIZES)
    w_main = w_in[:, :W_MAIN_WIDTH].astype(BF16)
    w_gate = w_in[:, W_GATE_OFF:].astype(BF16)
    w_dt = jnp.pad(w_in[:, W_DT_OFF:W_GATE_OFF], ((0, 0), (0, LANES - SSM_HEADS))).astype(BF16)
    return w_main, w_gate, w_dt


def _pad_lanes(v):
    return jnp.pad(v.astype(F32), (0, LANES - v.shape[0])).reshape(1, LANES)


def kernel(x, meta_tokens, g_pre, w_in, conv_w, conv_b, dt_bias, a_log, d_skip, attn_sinks, g_ssm_norm,
           w_out_att, w_out_ssm, w_out, g_post):
    b, seq, d = x.shape
    assert d == D_MODEL and PAD + N_META == BLOCK
    assert g_pre.shape[0] == 1, "one layer"
    in_tile, out_tile, blocks_per_step = 512, 512, 4
    assert seq % in_tile == 0 and seq % out_tile == 0 and seq % (blocks_per_step * BLOCK) == 0

    in_weights = _in_proj_weights(w_in[0])
    g_pre2 = g_pre[0].reshape(1, D_MODEL)
    conv_b2 = conv_b[0].reshape(1, CONV_DIM)
    ssd_params = (_pad_lanes(dt_bias[0]), _pad_lanes(a_log[0]),
                  jnp.repeat(d_skip[0].astype(F32), SSM_HEAD_DIM).reshape(1, SSM_INNER))

    h_meta = jnp.concatenate([jnp.zeros((PAD, D_MODEL), x.dtype), meta_tokens.astype(x.dtype)], axis=0)
    proj_m, dt_m, tail_m = _inproj(h_meta, g_pre2, *in_weights, conv_w[0], conv_b2,
                                   jnp.zeros((SUBLANES, CONV_DIM), F32),
                                   tm=BLOCK, tiles_per_seq=1, first_valid_row=PAD, emit_tail=True)
    (state_m,) = _ssd(proj_m.reshape(1, BLOCK, PACK_WIDTH), dt_m.reshape(1, BLOCK, LANES), *ssd_params,
                      jnp.zeros((SSM_STATE, SSM_INNER), F32), chunks_per_step=1, emit_y=False, emit_state=True)

    x2 = x.reshape(b * seq, D_MODEL)
    proj, dt, o_att = _inproj(x2, g_pre2, *in_weights, conv_w[0], conv_b2, tail_m,
                              (attn_sinks[0].astype(F32), *_meta_kv(proj_m)),
                              tm=in_tile, tiles_per_seq=seq // in_tile, first_valid_row=0, emit_tail=False)
    proj3 = proj.reshape(b, seq, PACK_WIDTH)
    (y_ssm,) = _ssd(proj3, dt.reshape(b, seq, LANES), *ssd_params, state_m,
                    chunks_per_step=blocks_per_step, emit_y=True, emit_state=False)
    out = _merge_out(o_att, y_ssm.reshape(b * seq, SSM_INNER), proj, x2,
                     g_ssm_norm[0].reshape(1, SSM_INNER), w_out_att[0].astype(BF16), w_out_ssm[0].astype(BF16),
                     w_out[0].astype(BF16), g_post[0].reshape(1, D_MODEL), tm=out_tile)
    return out.reshape(b, seq, D_MODEL)
```

```python
import functools
import math

import numpy as np
import jax
import jax.numpy as jnp
from jax import lax
from jax.experimental import pallas as pl
from jax.experimental.pallas import tpu as pltpu

F32 = jnp.float32
BF16 = jnp.bfloat16

D_MODEL = 1024
N_META = 16
BLOCK = 128
PAD = (-N_META) % BLOCK
NORM_EPS = 1e-6

ATT_HEAD_DIM = 64
ATT_Q_HEADS = D_MODEL // ATT_HEAD_DIM
ATT_KV_HEADS = 4
ATT_GROUP = ATT_Q_HEADS // ATT_KV_HEADS
ATT_WIDTH = ATT_Q_HEADS * ATT_HEAD_DIM
KV_WIDTH = ATT_KV_HEADS * ATT_HEAD_DIM

SSM_INNER = 2 * D_MODEL
SSM_HEAD_DIM = 64
SSM_HEADS = SSM_INNER // SSM_HEAD_DIM
SSM_GROUPS = 4
SSM_HEADS_PER_GROUP = SSM_HEADS // SSM_GROUPS
SSM_GROUP_WIDTH = SSM_INNER // SSM_GROUPS
SSM_STATE = 128
CONV_WIDTH = 4
CONV_DIM = SSM_INNER + 2 * SSM_GROUPS * SSM_STATE

SPLIT_SIZES = (ATT_WIDTH, KV_WIDTH, KV_WIDTH, ATT_WIDTH, SSM_INNER, CONV_DIM, SSM_HEADS, D_MODEL, D_MODEL)

LANES = 128
SUBLANES = 8
LOG2E = math.log2(math.e)

PACK_WIDTH = ATT_WIDTH + ATT_WIDTH + SSM_INNER + 2 * D_MODEL + CONV_DIM + 2 * KV_WIDTH
Q_BLK = 0
ZATT_BLK = 1
ZSSM_BLK = 1
GATE_BLK = 2
XBC_BLK = 2
XBC_OFF = XBC_BLK * CONV_DIM
K_BLK = (PACK_WIDTH - 2 * KV_WIDTH) // KV_WIDTH
V_BLK = K_BLK + 1
PROJ_CHUNK = 256
CONV_ROWS = 64
OUT_CHUNK = 256
PIECE_ROWS = 64
W_Q_OFF = 0
W_KV_OFF = ATT_WIDTH
W_ZATT_OFF = W_KV_OFF + 2 * KV_WIDTH
W_ZSSM_OFF = W_ZATT_OFF + ATT_WIDTH
W_XBC_OFF = W_ZSSM_OFF + SSM_INNER
W_MAIN_WIDTH = W_XBC_OFF + CONV_DIM
W_DT_OFF = W_MAIN_WIDTH
W_GATE_OFF = W_DT_OFF + SSM_HEADS
VMEM_LIMIT = 48 * 1024 * 1024
VMEM_LIMIT_IN_PROJ = 58 * 1024 * 1024
VMEM_LIMIT_SSD_MERGE = 56 * 1024 * 1024


def _sigmoid(x):
    return 0.5 * jnp.tanh(0.5 * x) + 0.5


def _silu(x):
    half = 0.5 * x
    return half * jnp.tanh(half) + half


def _const_spec(shape, n_grid):
    zeros = (0,) * len(shape)
    index_map = (lambda i: zeros) if n_grid == 1 else (lambda i, j: zeros)
    return pl.BlockSpec(shape, index_map, pipeline_mode=pl.Buffered(1))


def _inproj_kernel(h_ref, g_ref, w_ref, wgate_ref, wdt_ref, convw_ref, convb_ref, tail0_ref, *rest,
                   tiles_per_seq, first_valid_row, emit_tail, with_attention):
    rest = list(rest)
    if with_attention:
        sink_ref, km_ref, vmt_ref = rest[:3]
        del rest[:3]
    proj_ref, dt_ref = rest[:2]
    del rest[:2]
    tailout_ref = rest.pop(0) if emit_tail else None
    o_ref = rest.pop(0) if with_attention else None
    tail_scr, raw_scr = rest[:2]
    if with_attention:
        kd_scr, vtd_scr = rest[2:]
    tm = h_ref.shape[0]
    i = pl.program_id(0)
    first_tile = i % tiles_per_seq == 0

    @pl.when(first_tile)
    def _():
        tail_scr[...] = tail0_ref[...]

    h = h_ref[...]
    ms = jnp.mean(h * h, axis=-1, keepdims=True)
    u = (h * lax.rsqrt(ms + NORM_EPS) * g_ref[...]).astype(BF16)
    dt_ref[...] = jnp.dot(u, wdt_ref[...], preferred_element_type=F32)

    def project(src_ref, c0):
        return jnp.dot(u, src_ref[:, c0:c0 + PROJ_CHUNK], preferred_element_type=F32)

    def plain_chunk(src_ref, src_c0, out_c0):
        r = project(src_ref, src_c0)
        if out_c0 < ZATT_BLK * ATT_WIDTH:
            r = r * (ATT_HEAD_DIM ** -0.5)
        elif out_c0 < GATE_BLK * 2 * D_MODEL:
            r = _silu(r)
        elif out_c0 < XBC_OFF:
            r = _sigmoid(r)
        proj_ref[:, out_c0:out_c0 + PROJ_CHUNK] = r.astype(BF16)

    def xbc_matmul(k):
        cols = slice(k * PROJ_CHUNK, (k + 1) * PROJ_CHUNK)
        raw = raw_scr.at[k % 2]
        raw[0:SUBLANES, :] = tail_scr[:, cols]
        raw[SUBLANES:, :] = project(w_ref, W_XBC_OFF + k * PROJ_CHUNK)
        new_tail = raw[tm:, :]
        tail_scr[:, cols] = new_tail
        if emit_tail:
            tailout_ref[:, cols] = new_tail

    def conv_piece(k, r0):
        raw = raw_scr.at[k % 2]
        for l0 in range(0, PROJ_CHUNK, LANES):
            cols = slice(k * PROJ_CHUNK + l0, k * PROJ_CHUNK + l0 + LANES)
            w0, w1, w2, w3 = (convw_ref[t:t + 1, cols] for t in range(CONV_WIDTH))
            ext = raw[r0:r0 + SUBLANES + CONV_ROWS, l0:l0 + LANES]
            ext1 = pltpu.roll(ext, 1, axis=0)
            pair_old = w1 * ext + w0 * ext1
            acc = (convb_ref[:, cols] + w3 * ext[SUBLANES:] + w2 * ext1[SUBLANES:]
                   + pltpu.roll(pair_old, 2, axis=0)[SUBLANES:])
            xact = _silu(acc)
            if first_valid_row:
                rows = lax.broadcasted_iota(jnp.int32, (CONV_ROWS, 1), 0) + r0
                xact = jnp.where(rows >= first_valid_row, xact, 0.0)
            proj_ref[r0:r0 + CONV_ROWS, XBC_OFF + cols.start:XBC_OFF + cols.stop] = xact.astype(BF16)

    if with_attention:
        n_blocks = tm // BLOCK
        n_lanes = ATT_GROUP * BLOCK
        key = lax.broadcasted_iota(jnp.int32, (BLOCK, n_lanes), 0)
        qry = lax.broadcasted_iota(jnp.int32, (BLOCK, n_lanes), 1) % BLOCK
        in_cur = key <= qry
        rel = jnp.where(in_cur, qry - key, qry - key + BLOCK).astype(F32)
        rel_first = jnp.where(jnp.logical_or(in_cur, jnp.logical_not(first_tile)), rel, jnp.inf)
        cur_b = in_cur.astype(BF16)
        prev_b = 1.0 - cur_b
        low_lanes = lax.broadcasted_iota(jnp.int32, (1, LANES), 1) < ATT_HEAD_DIM
        low_rows = lax.broadcasted_iota(jnp.int32, (LANES, 1), 0) < ATT_HEAD_DIM
        nt = (((1,), (1,)), ((), ()))

        @pl.when(first_tile)
        def _():
            kd_scr[0] = jnp.zeros(kd_scr.shape[1:], BF16)
            vtd_scr[0] = jnp.zeros(vtd_scr.shape[1:], BF16)

        @pl.when(jnp.logical_not(first_tile))
        def _():
            kd_scr[0] = kd_scr[n_blocks]
            vtd_scr[0] = vtd_scr[n_blocks]

    def arrange_kv(blk):
        rows = slice(blk * BLOCK, (blk + 1) * BLOCK)
        k_f = proj_ref[rows, K_BLK * KV_WIDTH:(K_BLK + 1) * KV_WIDTH].astype(F32)
        v_f = proj_ref[rows, V_BLK * KV_WIDTH:(V_BLK + 1) * KV_WIDTH].astype(F32)
        for t in range(KV_WIDTH // LANES):
            k_tile = k_f[:, t * LANES:(t + 1) * LANES]
            k_swap = pltpu.roll(k_tile, ATT_HEAD_DIM, axis=1)
            kd_scr[blk + 1, 2 * t] = jnp.where(low_lanes, k_tile, k_swap).astype(BF16)
            kd_scr[blk + 1, 2 * t + 1] = jnp.where(low_lanes, k_swap, k_tile).astype(BF16)
            vt = v_f[:, t * LANES:(t + 1) * LANES].T
            vtd_scr[blk + 1, 2 * t] = jnp.concatenate([vt[:ATT_HEAD_DIM]] * 2, axis=0).astype(BF16)
            vtd_scr[blk + 1, 2 * t + 1] = jnp.concatenate([vt[ATT_HEAD_DIM:]] * 2, axis=0).astype(BF16)

    def attend(blk, g):
        rows = slice(blk * BLOCK, (blk + 1) * BLOCK)
        k_dup = jnp.concatenate([kd_scr[blk, g], kd_scr[blk + 1, g]], axis=0)
        vt_dup = jnp.concatenate([vtd_scr[blk, g], vtd_scr[blk + 1, g]], axis=1)
        q_rows = []
        for t in (2 * g, 2 * g + 1):
            q_tile = proj_ref[rows, Q_BLK * ATT_WIDTH + t * LANES:Q_BLK * ATT_WIDTH + (t + 1) * LANES]
            q_rows += [jnp.where(low_lanes, q_tile, 0), jnp.where(low_lanes, 0, q_tile)]
        q_cat = jnp.concatenate(q_rows, axis=0)
        s_band = lax.dot_general(k_dup, q_cat, nt, preferred_element_type=F32)
        s_meta = lax.dot_general(km_ref[g], q_cat, nt, preferred_element_type=F32)

        heads = range(g * ATT_GROUP, (g + 1) * ATT_GROUP)
        slopes = jnp.concatenate(
            [jnp.full((1, BLOCK), 2.0 ** (-8.0 * (h + 1) / ATT_Q_HEADS), F32) for h in heads], axis=1)
        sinks = jnp.concatenate([jnp.full((1, BLOCK), sink_ref[h], F32) for h in heads], axis=1)
        s = jnp.where(in_cur, s_band[BLOCK:], s_band[:BLOCK]) - slopes * (rel_first if blk == 0 else rel)
        m = jnp.maximum(jnp.max(s, axis=0, keepdims=True), jnp.max(s_meta, axis=0, keepdims=True))
        m = jnp.maximum(m, sinks)
        p = jnp.exp(s - m)
        p_meta = jnp.exp(s_meta - m)
        denom = (jnp.sum(p, axis=0, keepdims=True) + jnp.sum(p_meta, axis=0, keepdims=True)
                 + jnp.exp(sinks - m))
        p_b = p.astype(BF16)
        p_band = jnp.concatenate([p_b * prev_b, p_b * cur_b], axis=0)
        o_t = jnp.dot(vt_dup, p_band, preferred_element_type=F32)
        o_t = o_t + jnp.dot(vmt_ref[g], p_meta.astype(BF16), preferred_element_type=F32)
        o_t = o_t * (1.0 / denom)
        for pair in range(ATT_GROUP // 2):
            head_a = o_t[:, (2 * pair) * BLOCK:(2 * pair + 1) * BLOCK]
            head_b = o_t[:, (2 * pair + 1) * BLOCK:(2 * pair + 2) * BLOCK]
            tile = 2 * g + pair
            o_ref[rows, tile * LANES:(tile + 1) * LANES] = jnp.where(low_rows, head_a, head_b).T.astype(BF16)

    early, late = _plain_chunks(w_ref, wgate_ref)
    for chunk in early:
        plain_chunk(*chunk)
    attn_pending = []
    if with_attention:
        for blk in range(n_blocks):
            arrange_kv(blk)
        attn_pending = [(blk, g) for blk in range(n_blocks) for g in range(ATT_KV_HEADS)]
    n_xbc = CONV_DIM // PROJ_CHUNK
    n_items = n_xbc + len(late)
    xbc_at = {k * n_items // n_xbc: k for k in range(n_xbc)}
    late_iter = iter(late)
    matmuls = [xbc_at[n] if n in xbc_at else next(late_iter) for n in range(n_items)]
    conv_per_matmul = -(-n_xbc * (tm // CONV_ROWS) // (len(matmuls) - 1))
    attn_per_matmul = -(-len(attn_pending) // len(matmuls))
    conv_pending = []
    for item in matmuls:
        if isinstance(item, int):
            assert all(k > item - 2 for k, _ in conv_pending), "raw_scr slot still in use"
            xbc_matmul(item)
            conv_pending += [(item, r0) for r0 in range(0, tm, CONV_ROWS)]
        else:
            plain_chunk(*item)
        for piece in conv_pending[:conv_per_matmul]:
            conv_piece(*piece)
        conv_pending = conv_pending[conv_per_matmul:]
        for piece in attn_pending[:attn_per_matmul]:
            attend(*piece)
        attn_pending = attn_pending[attn_per_matmul:]
    for piece in conv_pending:
        conv_piece(*piece)
    for piece in attn_pending:
        attend(*piece)


def _plain_chunks(w_ref, wgate_ref):
    def chunks(ref, src, dst, width):
        return [(ref, src + c, dst + c) for c in range(0, width, PROJ_CHUNK)]

    early = chunks(w_ref, W_Q_OFF, Q_BLK * ATT_WIDTH, ATT_WIDTH) + chunks(w_ref, W_KV_OFF, K_BLK * KV_WIDTH, 2 * KV_WIDTH)
    late = (chunks(w_ref, W_ZATT_OFF, ZATT_BLK * ATT_WIDTH, ATT_WIDTH)
            + chunks(w_ref, W_ZSSM_OFF, ZSSM_BLK * SSM_INNER, SSM_INNER)
            + chunks(wgate_ref, 0, GATE_BLK * 2 * D_MODEL, 2 * D_MODEL))
    return early, late


def _inproj(h2, g_pre, w_main, w_gate, w_dt, conv_w, conv_b, tail0, attention=None, *, tm, tiles_per_seq,
            first_valid_row, emit_tail):
    rows = h2.shape[0]
    assert tm % CONV_ROWS == 0 and tm % BLOCK == 0
    operands = [h2, g_pre, w_main, w_gate, w_dt, conv_w, conv_b, tail0]
    in_specs = [
        pl.BlockSpec((tm, D_MODEL), lambda i: (i, 0)),
        _const_spec((1, D_MODEL), 1),
        _const_spec((D_MODEL, W_MAIN_WIDTH), 1),
        _const_spec((D_MODEL, 2 * D_MODEL), 1),
        _const_spec((D_MODEL, LANES), 1),
        _const_spec((CONV_WIDTH, CONV_DIM), 1),
        _const_spec((1, CONV_DIM), 1),
        _const_spec((SUBLANES, CONV_DIM), 1),
    ]
    out_specs = [pl.BlockSpec((tm, PACK_WIDTH), lambda i: (i, 0)), pl.BlockSpec((tm, LANES), lambda i: (i, 0))]
    out_shape = [jax.ShapeDtypeStruct((rows, PACK_WIDTH), BF16), jax.ShapeDtypeStruct((rows, LANES), F32)]
    scratch = [pltpu.VMEM((SUBLANES, CONV_DIM), F32),
               pltpu.VMEM((2, SUBLANES + tm, PROJ_CHUNK), F32)]
    if emit_tail:
        assert rows == tm
        out_specs.append(pl.BlockSpec((SUBLANES, CONV_DIM), lambda i: (0, 0)))
        out_shape.append(jax.ShapeDtypeStruct((SUBLANES, CONV_DIM), F32))
    if attention is not None:
        operands += list(attention)
        in_specs += [pl.BlockSpec(memory_space=pltpu.SMEM),
                     _const_spec((ATT_KV_HEADS, N_META, LANES), 1),
                     _const_spec((ATT_KV_HEADS, LANES, N_META), 1)]
        out_specs.append(pl.BlockSpec((tm, ATT_WIDTH), lambda i: (i, 0)))
        out_shape.append(jax.ShapeDtypeStruct((rows, ATT_WIDTH), BF16))
        kv_slots = (tm // BLOCK + 1, ATT_KV_HEADS, BLOCK, LANES)
        scratch += [pltpu.VMEM(kv_slots, BF16), pltpu.VMEM(kv_slots, BF16)]
    return pl.pallas_call(
        functools.partial(_inproj_kernel, tiles_per_seq=tiles_per_seq, first_valid_row=first_valid_row,
                          emit_tail=emit_tail, with_attention=attention is not None),
        grid=(rows // tm,),
        in_specs=in_specs,
        out_specs=out_specs,
        out_shape=out_shape,
        scratch_shapes=scratch,
        compiler_params=pltpu.CompilerParams(dimension_semantics=("arbitrary",),
                                             vmem_limit_bytes=VMEM_LIMIT_IN_PROJ),
        name="in_proj",
    )(*operands)


def _meta_kv(proj_meta):
    k_off, v_off = K_BLK * KV_WIDTH, V_BLK * KV_WIDTH
    k_dup, vt_dup = [], []
    for g in range(ATT_KV_HEADS):
        kg = proj_meta[PAD:, k_off + g * ATT_HEAD_DIM:k_off + (g + 1) * ATT_HEAD_DIM]
        vg = proj_meta[PAD:, v_off + g * ATT_HEAD_DIM:v_off + (g + 1) * ATT_HEAD_DIM]
        k_dup.append(jnp.concatenate([kg, kg], axis=1))
        vt_dup.append(jnp.concatenate([vg.T, vg.T], axis=0))
    return jnp.stack(k_dup), jnp.stack(vt_dup)


def _split3(a):
    hi = a.astype(BF16).astype(F32)
    r1 = a - hi
    mid = r1.astype(BF16).astype(F32)
    return hi, mid, r1 - mid


def _expand_lhs(x):
    hi, mid, lo = _split3(x)
    return (hi + pltpu.roll(mid, SSM_HEADS, axis=1) + pltpu.roll(lo, 2 * SSM_HEADS, axis=1)).astype(BF16)


def _expand_matrix():
    r = np.arange(LANES)[:, None]
    c = np.arange(SSM_INNER)[None, :]
    return jnp.asarray((r < 3 * SSM_HEADS) & (r % SSM_HEADS == c // SSM_HEAD_DIM), BF16)


def _ssd_chunk_fns(xact_ref, dt_ref, dtb_ref, alog_ref, dskip_ref, expand_ref, state_scr, q2row_scr, y_ref):
    emit_y = y_ref is not None
    row = lax.broadcasted_iota(jnp.int32, (BLOCK, BLOCK), 0)
    col = lax.broadcasted_iota(jnp.int32, (BLOCK, BLOCK), 1)
    tril = row >= col
    tril_b = tril.astype(BF16)
    lane = lax.broadcasted_iota(jnp.int32, (1, LANES), 1)
    head_lane = lane < SSM_HEADS
    low_b = (lane < SSM_HEAD_DIM).astype(BF16)
    high_b = 1.0 - low_b
    neg_a2 = -jnp.exp(alog_ref[...]) * LOG2E
    nt = (((1,), (1,)), ((), ()))
    n_bc = SSM_GROUPS * SSM_STATE

    def decays(ch):
        rows = slice(ch * BLOCK, (ch + 1) * BLOCK)
        x_dt = dt_ref[rows, :] + dtb_ref[...]
        dt = jnp.maximum(x_dt, 0.0) + jnp.log1p(jnp.exp(-jnp.abs(x_dt)))
        a2 = dt * neg_a2
        cs3 = jnp.dot(tril_b, jnp.concatenate(_split3(a2), axis=1).astype(BF16), preferred_element_type=F32)
        cs2 = cs3[:, :LANES] + cs3[:, LANES:2 * LANES] + cs3[:, 2 * LANES:]
        q2 = cs2 - jnp.log(dt) * LOG2E
        cs2_last = cs2[BLOCK - 1:BLOCK, :]
        q2row_scr[ch] = q2.T
        w = jnp.where(head_lane, jnp.exp2(cs2_last - q2), 0.0)
        cd = jnp.where(head_lane, jnp.exp2(cs2_last), 0.0)
        parts = [_expand_lhs(w), _expand_lhs(jnp.broadcast_to(cd, (2 * SUBLANES, LANES)))]
        if emit_y:
            parts.insert(0, _expand_lhs(jnp.where(head_lane, jnp.exp2(cs2), 0.0)))
        ex = jnp.dot(jnp.concatenate(parts, axis=0), expand_ref[...], preferred_element_type=F32)
        off = BLOCK if emit_y else 0
        e_exp = ex[:BLOCK] if emit_y else None
        return cs2, e_exp, ex[off:off + BLOCK], ex[off + BLOCK:off + BLOCK + 1]

    def group(ch, g, cs2, e_exp, w_exp, cd_exp):
        rows = slice(ch * BLOCK, (ch + 1) * BLOCK)
        gcols = slice(g * SSM_GROUP_WIDTH, (g + 1) * SSM_GROUP_WIDTH)
        bg = xact_ref[rows,SSM_INNER + g * SSM_STATE:SSM_INNER + (g + 1) * SSM_STATE]
        cg = xact_ref[rows,SSM_INNER + n_bc + g * SSM_STATE:SSM_INNER + n_bc + (g + 1) * SSM_STATE]
        xs_f = xact_ref[rows,gcols].astype(F32)
        s_prev = state_scr[:, gcols]
        bg_t = bg.astype(F32).T.astype(BF16)
        new_states = jnp.dot(bg_t, (xs_f * w_exp[:, gcols]).astype(BF16), preferred_element_type=F32)
        state_scr[:, gcols] = s_prev * cd_exp[:, gcols] + new_states
        if not emit_y:
            return
        cb = lax.dot_general(cg, bg, nt, preferred_element_type=F32)
        y_off = jnp.dot(cg, s_prev.astype(BF16), preferred_element_type=F32)
        for pair in range(SSM_HEADS_PER_GROUP // 2):
            h0 = g * SSM_HEADS_PER_GROUP + 2 * pair
            cols = slice(h0 * SSM_HEAD_DIM, (h0 + 2) * SSM_HEAD_DIM)
            pcols = slice(2 * pair * SSM_HEAD_DIM, (2 * pair + 2) * SSM_HEAD_DIM)
            m_parts = []
            for h in (h0, h0 + 1):
                expo = jnp.where(tril, cs2[:, h:h + 1] - q2row_scr[ch, h:h + 1, :], -jnp.inf)
                m_parts.append((cb * jnp.exp2(expo)).astype(BF16))
            x_pair = xact_ref[rows,cols]
            x_bd = jnp.concatenate([x_pair * low_b, x_pair * high_b], axis=0)
            y_pair = jnp.dot(jnp.concatenate(m_parts, axis=1), x_bd, preferred_element_type=F32)
            y_pair = y_pair + e_exp[:, cols] * y_off[:, pcols] + dskip_ref[:, cols] * xs_f[:, pcols]
            y_ref[rows, cols] = y_pair.astype(y_ref.dtype)

    return decays, group


def _ssd_state_kernel(xact_ref, dt_ref, dtb_ref, alog_ref, dskip_ref, expand_ref, sout_ref, q2row_scr):
    sout_ref[...] = jnp.zeros(sout_ref.shape, F32)
    decays, group = _ssd_chunk_fns(xact_ref, dt_ref, dtb_ref, alog_ref, dskip_ref, expand_ref, sout_ref,
                                   q2row_scr, None)
    terms = decays(0)
    for g in range(SSM_GROUPS):
        group(0, g, *terms)


def _ssd_leading_state(proj_m, dt_m, dt_bias, a_log, d_skip):
    return pl.pallas_call(
        _ssd_state_kernel,
        grid=(1,),
        in_specs=[
            pl.BlockSpec((BLOCK, CONV_DIM), lambda i: (0, XBC_BLK)),
            _const_spec((BLOCK, LANES), 1),
            _const_spec((1, LANES), 1),
            _const_spec((1, LANES), 1),
            _const_spec((1, SSM_INNER), 1),
            _const_spec((LANES, SSM_INNER), 1),
        ],
        out_specs=pl.BlockSpec((SSM_STATE, SSM_INNER), lambda i: (0, 0)),
        out_shape=jax.ShapeDtypeStruct((SSM_STATE, SSM_INNER), F32),
        scratch_shapes=[pltpu.VMEM((1, LANES, BLOCK), F32)],
        compiler_params=pltpu.CompilerParams(vmem_limit_bytes=VMEM_LIMIT),
        name="ssd_leading_state",
    )(proj_m, dt_m, dt_bias, a_log, d_skip, _expand_matrix())


def _ssd_merge_kernel(xact_ref, dt_ref, zssm_ref, dtb_ref, alog_ref, dskip_ref, expand_ref, s0_ref, gnorm_ref,
                      o_ref, zatt_ref, gate_ref, x_ref, woa_ref, wos_ref, wo_ref, gpost_ref,
                      out_ref, state_scr, q2row_scr, y_scr, ssm_scr, att_scr, ya_scr, merged_scr,
                      *, tiles_per_seq, n_tiles):
    tm = x_ref.shape[0]
    i = pl.program_id(0)
    scan_tile = jnp.minimum(i, n_tiles - 1)
    scan_slot = i % 2
    merge_slot = (i + 1) % 2

    @pl.when(scan_tile % tiles_per_seq == 0)
    def _():
        state_scr[...] = s0_ref[...]

    @pl.when(i == 0)
    def _():
        ssm_scr[1] = jnp.zeros(ssm_scr.shape[1:], BF16)

    decays, group = _ssd_chunk_fns(xact_ref, dt_ref, dtb_ref, alog_ref, dskip_ref, expand_ref, state_scr,
                                   q2row_scr, y_scr)

    def branch_piece(ch):
        rows = slice(ch * BLOCK, (ch + 1) * BLOCK)
        y = y_scr[rows, :] * zssm_ref[rows, :].astype(F32)
        for g in range(SSM_GROUPS):
            cols = slice(g * SSM_GROUP_WIDTH, (g + 1) * SSM_GROUP_WIDTH)
            yg = y[:, cols]
            yg = yg * lax.rsqrt(jnp.mean(yg * yg, axis=-1, keepdims=True) + NORM_EPS)
            ssm_scr[scan_slot, rows, cols] = (yg * gnorm_ref[:, cols]).astype(BF16)

    def att_piece(r0):
        rows = slice(r0, r0 + PIECE_ROWS)
        att_scr[rows, :] = (o_ref[rows, :].astype(F32) * zatt_ref[rows, :].astype(F32)).astype(BF16)

    def att_chunk(c0):
        cols = slice(c0, c0 + OUT_CHUNK)
        ya_scr[:, cols] = jnp.dot(att_scr[...], woa_ref[:, cols], preferred_element_type=F32)

    def ssm_merge_chunk(c0):
        cols = slice(c0, c0 + OUT_CHUNK)
        ys = jnp.dot(ssm_scr[merge_slot], wos_ref[:, cols], preferred_element_type=F32)
        gate_att = gate_ref[:, cols].astype(F32)
        gate_ssm = gate_ref[:, D_MODEL + c0:D_MODEL + c0 + OUT_CHUNK].astype(F32)
        merged_scr[:, cols] = (gate_att * ya_scr[:, cols] + gate_ssm * ys).astype(BF16)

    def out_chunk(c0):
        cols = slice(c0, c0 + OUT_CHUNK)
        out_ref[:, cols] = jnp.dot(merged_scr[...], wo_ref[:, cols], preferred_element_type=F32)

    def finish():
        out = out_ref[...]
        ms = jnp.mean(out * out, axis=-1, keepdims=True)
        out_ref[...] = x_ref[...] + out * lax.rsqrt(ms + NORM_EPS) * gpost_ref[...]

    chunks = range(0, D_MODEL, OUT_CHUNK)
    pieces = [functools.partial(att_piece, r0) for r0 in range(0, tm, PIECE_ROWS)]
    matmuls = ([functools.partial(att_chunk, c0) for c0 in chunks]
               + [functools.partial(ssm_merge_chunk, c0) for c0 in chunks]
               + [functools.partial(out_chunk, c0) for c0 in chunks] + [finish])

    n_chunks = tm // BLOCK
    n_slots = n_chunks * SSM_GROUPS
    lead = 2
    per_lead = -(-len(pieces) // lead)
    per_slot = -(-len(matmuls) // (n_slots - lead))
    terms = decays(0)
    slot = 0
    for ch in range(n_chunks):
        next_terms = None
        for g in range(SSM_GROUPS):
            group(ch, g, *terms)
            if g == SSM_GROUPS // 2 - 1 and ch + 1 < n_chunks:
                next_terms = decays(ch + 1)
            if slot < lead:
                todo, pieces = pieces[:per_lead], pieces[per_lead:]
            else:
                todo, matmuls = matmuls[:per_slot], matmuls[per_slot:]
            for piece in todo:
                piece()
            slot += 1
        branch_piece(ch)
        terms = next_terms
    for piece in pieces + matmuls:
        piece()


def _ssd_merge(proj, dt, o_att, x2, dt_bias, a_log, d_skip, s0, g_norm, w_out_att, w_out_ssm, w_out, g_post,
               *, tm, tiles_per_seq):
    rows = x2.shape[0]
    n_tiles = rows // tm
    scan = lambda col: (lambda i: (jnp.minimum(i, n_tiles - 1), col))
    merge = lambda col: (lambda i: (jnp.maximum(i - 1, 0), col))
    return pl.pallas_call(
        functools.partial(_ssd_merge_kernel, tiles_per_seq=tiles_per_seq, n_tiles=n_tiles),
        grid=(n_tiles + 1,),
        in_specs=[
            pl.BlockSpec((tm, CONV_DIM), scan(XBC_BLK)),
            pl.BlockSpec((tm, LANES), scan(0)),
            pl.BlockSpec((tm, SSM_INNER), scan(ZSSM_BLK)),
            _const_spec((1, LANES), 1),
            _const_spec((1, LANES), 1),
            _const_spec((1, SSM_INNER), 1),
            _const_spec((LANES, SSM_INNER), 1),
            _const_spec((SSM_STATE, SSM_INNER), 1),
            _const_spec((1, SSM_INNER), 1),
            pl.BlockSpec((tm, ATT_WIDTH), merge(0)),
            pl.BlockSpec((tm, ATT_WIDTH), merge(ZATT_BLK)),
            pl.BlockSpec((tm, 2 * D_MODEL), merge(GATE_BLK)),
            pl.BlockSpec((tm, D_MODEL), merge(0)),
            _const_spec((ATT_WIDTH, D_MODEL), 1),
            _const_spec((SSM_INNER, D_MODEL), 1),
            _const_spec((D_MODEL, D_MODEL), 1),
            _const_spec((1, D_MODEL), 1),
        ],
        out_specs=pl.BlockSpec((tm, D_MODEL), merge(0)),
        out_shape=jax.ShapeDtypeStruct((rows, D_MODEL), F32),
        scratch_shapes=[
            pltpu.VMEM((SSM_STATE, SSM_INNER), F32),
            pltpu.VMEM((tm // BLOCK, LANES, BLOCK), F32),
            pltpu.VMEM((tm, SSM_INNER), F32),
            pltpu.VMEM((2, tm, SSM_INNER), BF16),
            pltpu.VMEM((tm, ATT_WIDTH), BF16),
            pltpu.VMEM((tm, D_MODEL), F32),
            pltpu.VMEM((tm, D_MODEL), BF16),
        ],
        compiler_params=pltpu.CompilerParams(dimension_semantics=("arbitrary",),
                                             vmem_limit_bytes=VMEM_LIMIT_SSD_MERGE),
        name="ssd_merge",
    )(proj, dt, proj, dt_bias, a_log, d_skip, _expand_matrix(), s0, g_norm,
      o_att, proj, proj, x2, w_out_att, w_out_ssm, w_out, g_post)


def _cast_kernel(w_ref, o_ref):
    o_ref[...] = w_ref[...].astype(o_ref.dtype)


def _cast_leading_columns(w, width, dtype, *, n_steps):
    rows = w.shape[0]
    assert width % (n_steps * LANES) == 0
    spec = pl.BlockSpec((rows, width // n_steps), lambda j: (0, j))
    return pl.pallas_call(_cast_kernel, grid=(n_steps,), in_specs=[spec], out_specs=spec,
                          out_shape=jax.ShapeDtypeStruct((rows, width), dtype),
                          compiler_params=pltpu.CompilerParams(vmem_limit_bytes=VMEM_LIMIT),
                          name="cast_weight")(w)


def _in_proj_weights(w_in):
    assert w_in.shape[1] == W_GATE_OFF + 2 * D_MODEL == sum(SPLIT_SIZES)
    w_main = _cast_leading_columns(w_in, W_MAIN_WIDTH, BF16, n_steps=5)
    w_gate = w_in[:, W_GATE_OFF:].astype(BF16)
    w_dt = jnp.pad(w_in[:, W_DT_OFF:W_GATE_OFF], ((0, 0), (0, LANES - SSM_HEADS))).astype(BF16)
    return w_main, w_gate, w_dt


def _pad_lanes(v):
    return jnp.pad(v.astype(F32), (0, LANES - v.shape[0])).reshape(1, LANES)


def kernel(x, meta_tokens, g_pre, w_in, conv_w, conv_b, dt_bias, a_log, d_skip, attn_sinks, g_ssm_norm,
           w_out_att, w_out_ssm, w_out, g_post):
    b, seq, d = x.shape
    assert d == D_MODEL and PAD + N_META == BLOCK
    assert g_pre.shape[0] == 1, "one layer"
    in_tile, out_tile = 512, 512
    assert seq % in_tile == 0 and seq % out_tile == 0

    in_weights = _in_proj_weights(w_in[0])
    g_pre2 = g_pre[0].reshape(1, D_MODEL)
    conv_b2 = conv_b[0].reshape(1, CONV_DIM)
    ssd_params = (_pad_lanes(dt_bias[0]), _pad_lanes(a_log[0]),
                  jnp.repeat(d_skip[0].astype(F32), SSM_HEAD_DIM).reshape(1, SSM_INNER))

    h_meta = jnp.concatenate([jnp.zeros((PAD, D_MODEL), x.dtype), meta_tokens.astype(x.dtype)], axis=0)
    proj_m, dt_m, tail_m = _inproj(h_meta, g_pre2, *in_weights, conv_w[0], conv_b2,
                                   jnp.zeros((SUBLANES, CONV_DIM), F32),
                                   tm=BLOCK, tiles_per_seq=1, first_valid_row=PAD, emit_tail=True)
    state_m = _ssd_leading_state(proj_m, dt_m, *ssd_params)

    x2 = x.reshape(b * seq, D_MODEL)
    proj, dt, o_att = _inproj(x2, g_pre2, *in_weights, conv_w[0], conv_b2, tail_m,
                              (attn_sinks[0].astype(F32), *_meta_kv(proj_m)),
                              tm=in_tile, tiles_per_seq=seq // in_tile, first_valid_row=0, emit_tail=False)
    out = _ssd_merge(proj, dt, o_att, x2, *ssd_params, state_m, g_ssm_norm[0].reshape(1, SSM_INNER),
                     w_out_att[0].astype(BF16), w_out_ssm[0].astype(BF16), w_out[0].astype(BF16),
                     g_post[0].reshape(1, D_MODEL), tm=out_tile, tiles_per_seq=seq // out_tile)
    return out.reshape(b, seq, D_MODEL)
```

```python
import functools
import math

import numpy as np
import jax
import jax.numpy as jnp
from jax import lax
from jax.experimental import pallas as pl
from jax.experimental.pallas import tpu as pltpu

F32 = jnp.float32
BF16 = jnp.bfloat16

D_MODEL = 1024
N_META = 16
BLOCK = 128
PAD = (-N_META) % BLOCK
NORM_EPS = 1e-6

ATT_HEAD_DIM = 64
ATT_Q_HEADS = D_MODEL // ATT_HEAD_DIM
ATT_KV_HEADS = 4
ATT_GROUP = ATT_Q_HEADS // ATT_KV_HEADS
ATT_WIDTH = ATT_Q_HEADS * ATT_HEAD_DIM
KV_WIDTH = ATT_KV_HEADS * ATT_HEAD_DIM

SSM_INNER = 2 * D_MODEL
SSM_HEAD_DIM = 64
SSM_HEADS = SSM_INNER // SSM_HEAD_DIM
SSM_GROUPS = 4
SSM_HEADS_PER_GROUP = SSM_HEADS // SSM_GROUPS
SSM_GROUP_WIDTH = SSM_INNER // SSM_GROUPS
SSM_STATE = 128
CONV_WIDTH = 4
CONV_DIM = SSM_INNER + 2 * SSM_GROUPS * SSM_STATE

SPLIT_SIZES = (ATT_WIDTH, KV_WIDTH, KV_WIDTH, ATT_WIDTH, SSM_INNER, CONV_DIM, SSM_HEADS, D_MODEL, D_MODEL)

LANES = 128
SUBLANES = 8
LOG2E = math.log2(math.e)

PACK_WIDTH = ATT_WIDTH + ATT_WIDTH + SSM_INNER + 2 * D_MODEL + CONV_DIM + 2 * KV_WIDTH
Q_BLK = 0
ZATT_BLK = 1
ZSSM_BLK = 1
GATE_BLK = 2
XBC_BLK = 2
XBC_OFF = XBC_BLK * CONV_DIM
K_BLK = (PACK_WIDTH - 2 * KV_WIDTH) // KV_WIDTH
V_BLK = K_BLK + 1
PROJ_CHUNK = 256
CONV_ROWS = 64
OUT_CHUNK = 256
PIECE_ROWS = 64
W_Q_OFF = 0
W_KV_OFF = ATT_WIDTH
W_ZATT_OFF = W_KV_OFF + 2 * KV_WIDTH
W_ZSSM_OFF = W_ZATT_OFF + ATT_WIDTH
W_XBC_OFF = W_ZSSM_OFF + SSM_INNER
W_MAIN_WIDTH = W_XBC_OFF + CONV_DIM
W_DT_OFF = W_MAIN_WIDTH
W_GATE_OFF = W_DT_OFF + SSM_HEADS
VMEM_LIMIT = 48 * 1024 * 1024
VMEM_LIMIT_IN_PROJ = 58 * 1024 * 1024
VMEM_LIMIT_SSD_MERGE = 56 * 1024 * 1024


def _sigmoid(x):
    return 0.5 * jnp.tanh(0.5 * x) + 0.5


def _silu(x):
    half = 0.5 * x
    return half * jnp.tanh(half) + half


def _const_spec(shape, n_grid):
    zeros = (0,) * len(shape)
    index_map = (lambda i: zeros) if n_grid == 1 else (lambda i, j: zeros)
    return pl.BlockSpec(shape, index_map, pipeline_mode=pl.Buffered(1))


def _inproj_kernel(h_ref, g_ref, w_ref, wgate_ref, wdt_ref, convw_ref, convb_ref, tail0_ref, *rest,
                   tiles_per_seq, first_valid_row, emit_tail, with_attention):
    rest = list(rest)
    if with_attention:
        sink_ref, km_ref, vmt_ref = rest[:3]
        del rest[:3]
    proj_ref, dt_ref = rest[:2]
    del rest[:2]
    tailout_ref = rest.pop(0) if emit_tail else None
    o_ref = rest.pop(0) if with_attention else None
    tail_scr, raw_scr = rest[:2]
    if with_attention:
        kd_scr, vtd_scr = rest[2:]
    tm = h_ref.shape[0]
    i = pl.program_id(0)
    first_tile = i % tiles_per_seq == 0

    @pl.when(first_tile)
    def _():
        tail_scr[...] = tail0_ref[...]

    h = h_ref[...]
    ms = jnp.mean(h * h, axis=-1, keepdims=True)
    u = (h * lax.rsqrt(ms + NORM_EPS) * g_ref[...]).astype(BF16)
    dt_ref[...] = jnp.dot(u, wdt_ref[...], preferred_element_type=F32)

    def project(src_ref, c0):
        return jnp.dot(u, src_ref[:, c0:c0 + PROJ_CHUNK], preferred_element_type=F32)

    def plain_chunk(src_ref, src_c0, out_c0):
        r = project(src_ref, src_c0)
        if out_c0 < ZATT_BLK * ATT_WIDTH:
            r = r * (ATT_HEAD_DIM ** -0.5)
        elif out_c0 < GATE_BLK * 2 * D_MODEL:
            r = _silu(r)
        elif out_c0 < XBC_OFF:
            r = _sigmoid(r)
        proj_ref[:, out_c0:out_c0 + PROJ_CHUNK] = r.astype(BF16)

    def xbc_matmul(k):
        cols = slice(k * PROJ_CHUNK, (k + 1) * PROJ_CHUNK)
        raw = raw_scr.at[k % 2]
        raw[0:SUBLANES, :] = tail_scr[:, cols]
        raw[SUBLANES:, :] = project(w_ref, W_XBC_OFF + k * PROJ_CHUNK)
        new_tail = raw[tm:, :]
        tail_scr[:, cols] = new_tail
        if emit_tail:
            tailout_ref[:, cols] = new_tail

    def conv_piece(k, r0):
        raw = raw_scr.at[k % 2]
        for l0 in range(0, PROJ_CHUNK, LANES):
            cols = slice(k * PROJ_CHUNK + l0, k * PROJ_CHUNK + l0 + LANES)
            w0, w1, w2, w3 = (convw_ref[t:t + 1, cols] for t in range(CONV_WIDTH))
            ext = raw[r0:r0 + SUBLANES + CONV_ROWS, l0:l0 + LANES]
            ext1 = pltpu.roll(ext, 1, axis=0)
            pair_old = w1 * ext + w0 * ext1
            acc = (convb_ref[:, cols] + w3 * ext[SUBLANES:] + w2 * ext1[SUBLANES:]
                   + pltpu.roll(pair_old, 2, axis=0)[SUBLANES:])
            xact = _silu(acc)
            if first_valid_row:
                rows = lax.broadcasted_iota(jnp.int32, (CONV_ROWS, 1), 0) + r0
                xact = jnp.where(rows >= first_valid_row, xact, 0.0)
            proj_ref[r0:r0 + CONV_ROWS, XBC_OFF + cols.start:XBC_OFF + cols.stop] = xact.astype(BF16)

    if with_attention:
        n_blocks = tm // BLOCK
        n_lanes = ATT_GROUP * BLOCK
        key = lax.broadcasted_iota(jnp.int32, (BLOCK, n_lanes), 0)
        qry = lax.broadcasted_iota(jnp.int32, (BLOCK, n_lanes), 1) % BLOCK
        in_cur = key <= qry
        rel = jnp.where(in_cur, qry - key, qry - key + BLOCK).astype(F32)
        rel_first = jnp.where(jnp.logical_or(in_cur, jnp.logical_not(first_tile)), rel, jnp.inf)
        cur_b = in_cur.astype(BF16)
        prev_b = 1.0 - cur_b
        low_lanes = lax.broadcasted_iota(jnp.int32, (1, LANES), 1) < ATT_HEAD_DIM
        low_rows = lax.broadcasted_iota(jnp.int32, (LANES, 1), 0) < ATT_HEAD_DIM
        nt = (((1,), (1,)), ((), ()))

        @pl.when(first_tile)
        def _():
            kd_scr[0] = jnp.zeros(kd_scr.shape[1:], BF16)
            vtd_scr[0] = jnp.zeros(vtd_scr.shape[1:], BF16)

        @pl.when(jnp.logical_not(first_tile))
        def _():
            kd_scr[0] = kd_scr[n_blocks]
            vtd_scr[0] = vtd_scr[n_blocks]

    def arrange_kv(blk):
        rows = slice(blk * BLOCK, (blk + 1) * BLOCK)
        k_f = proj_ref[rows, K_BLK * KV_WIDTH:(K_BLK + 1) * KV_WIDTH].astype(F32)
        v_f = proj_ref[rows, V_BLK * KV_WIDTH:(V_BLK + 1) * KV_WIDTH].astype(F32)
        for t in range(KV_WIDTH // LANES):
            k_tile = k_f[:, t * LANES:(t + 1) * LANES]
            k_swap = pltpu.roll(k_tile, ATT_HEAD_DIM, axis=1)
            kd_scr[blk + 1, 2 * t] = jnp.where(low_lanes, k_tile, k_swap).astype(BF16)
            kd_scr[blk + 1, 2 * t + 1] = jnp.where(low_lanes, k_swap, k_tile).astype(BF16)
            vt = v_f[:, t * LANES:(t + 1) * LANES].T
            vtd_scr[blk + 1, 2 * t] = jnp.concatenate([vt[:ATT_HEAD_DIM]] * 2, axis=0).astype(BF16)
            vtd_scr[blk + 1, 2 * t + 1] = jnp.concatenate([vt[ATT_HEAD_DIM:]] * 2, axis=0).astype(BF16)

    def attend(blk, g):
        rows = slice(blk * BLOCK, (blk + 1) * BLOCK)
        k_dup = jnp.concatenate([kd_scr[blk, g], kd_scr[blk + 1, g]], axis=0)
        vt_dup = jnp.concatenate([vtd_scr[blk, g], vtd_scr[blk + 1, g]], axis=1)
        q_rows = []
        for t in (2 * g, 2 * g + 1):
            q_tile = proj_ref[rows, Q_BLK * ATT_WIDTH + t * LANES:Q_BLK * ATT_WIDTH + (t + 1) * LANES]
            q_rows += [jnp.where(low_lanes, q_tile, 0), jnp.where(low_lanes, 0, q_tile)]
        q_cat = jnp.concatenate(q_rows, axis=0)
        s_band = lax.dot_general(k_dup, q_cat, nt, preferred_element_type=F32)
        s_meta = lax.dot_general(km_ref[g], q_cat, nt, preferred_element_type=F32)

        heads = range(g * ATT_GROUP, (g + 1) * ATT_GROUP)
        slopes = jnp.concatenate(
            [jnp.full((1, BLOCK), 2.0 ** (-8.0 * (h + 1) / ATT_Q_HEADS), F32) for h in heads], axis=1)
        sinks = jnp.concatenate([jnp.full((1, BLOCK), sink_ref[h], F32) for h in heads], axis=1)
        s = jnp.where(in_cur, s_band[BLOCK:], s_band[:BLOCK]) - slopes * (rel_first if blk == 0 else rel)
        m = jnp.maximum(jnp.max(s, axis=0, keepdims=True), jnp.max(s_meta, axis=0, keepdims=True))
        m = jnp.maximum(m, sinks)
        p = jnp.exp(s - m)
        p_meta = jnp.exp(s_meta - m)
        denom = (jnp.sum(p, axis=0, keepdims=True) + jnp.sum(p_meta, axis=0, keepdims=True)
                 + jnp.exp(sinks - m))
        p_b = p.astype(BF16)
        p_band = jnp.concatenate([p_b * prev_b, p_b * cur_b], axis=0)
        o_t = jnp.dot(vt_dup, p_band, preferred_element_type=F32)
        o_t = o_t + jnp.dot(vmt_ref[g], p_meta.astype(BF16), preferred_element_type=F32)
        o_t = o_t * (1.0 / denom)
        for pair in range(ATT_GROUP // 2):
            head_a = o_t[:, (2 * pair) * BLOCK:(2 * pair + 1) * BLOCK]
            head_b = o_t[:, (2 * pair + 1) * BLOCK:(2 * pair + 2) * BLOCK]
            tile = 2 * g + pair
            o_ref[rows, tile * LANES:(tile + 1) * LANES] = jnp.where(low_rows, head_a, head_b).T.astype(BF16)

    early, late = _plain_chunks(w_ref, wgate_ref)
    for chunk in early:
        plain_chunk(*chunk)
    attn_pending = []
    if with_attention:
        for blk in range(n_blocks):
            arrange_kv(blk)
        attn_pending = [(blk, g) for blk in range(n_blocks) for g in range(ATT_KV_HEADS)]
    n_xbc = CONV_DIM // PROJ_CHUNK
    n_items = n_xbc + len(late)
    xbc_at = {k * n_items // n_xbc: k for k in range(n_xbc)}
    late_iter = iter(late)
    matmuls = [xbc_at[n] if n in xbc_at else next(late_iter) for n in range(n_items)]
    conv_per_matmul = -(-n_xbc * (tm // CONV_ROWS) // (len(matmuls) - 1))
    attn_per_matmul = -(-len(attn_pending) // len(matmuls))
    conv_pending = []
    for item in matmuls:
        if isinstance(item, int):
            assert all(k > item - 2 for k, _ in conv_pending), "raw_scr slot still in use"
            xbc_matmul(item)
            conv_pending += [(item, r0) for r0 in range(0, tm, CONV_ROWS)]
        else:
            plain_chunk(*item)
        for piece in conv_pending[:conv_per_matmul]:
            conv_piece(*piece)
        conv_pending = conv_pending[conv_per_matmul:]
        for piece in attn_pending[:attn_per_matmul]:
            attend(*piece)
        attn_pending = attn_pending[attn_per_matmul:]
    for piece in conv_pending:
        conv_piece(*piece)
    for piece in attn_pending:
        attend(*piece)


def _plain_chunks(w_ref, wgate_ref):
    def chunks(ref, src, dst, width):
        return [(ref, src + c, dst + c) for c in range(0, width, PROJ_CHUNK)]

    early = chunks(w_ref, W_Q_OFF, Q_BLK * ATT_WIDTH, ATT_WIDTH) + chunks(w_ref, W_KV_OFF, K_BLK * KV_WIDTH, 2 * KV_WIDTH)
    late = (chunks(w_ref, W_ZATT_OFF, ZATT_BLK * ATT_WIDTH, ATT_WIDTH)
            + chunks(w_ref, W_ZSSM_OFF, ZSSM_BLK * SSM_INNER, SSM_INNER)
            + chunks(wgate_ref, 0, GATE_BLK * 2 * D_MODEL, 2 * D_MODEL))
    return early, late


def _inproj(h2, g_pre, w_main, w_gate, w_dt, conv_w, conv_b, tail0, attention=None, *, tm, tiles_per_seq,
            first_valid_row, emit_tail):
    rows = h2.shape[0]
    assert tm % CONV_ROWS == 0 and tm % BLOCK == 0
    operands = [h2, g_pre, w_main, w_gate, w_dt, conv_w, conv_b, tail0]
    in_specs = [
        pl.BlockSpec((tm, D_MODEL), lambda i: (i, 0)),
        _const_spec((1, D_MODEL), 1),
        _const_spec((D_MODEL, W_MAIN_WIDTH), 1),
        _const_spec((D_MODEL, 2 * D_MODEL), 1),
        _const_spec((D_MODEL, LANES), 1),
        _const_spec((CONV_WIDTH, CONV_DIM), 1),
        _const_spec((1, CONV_DIM), 1),
        _const_spec((SUBLANES, CONV_DIM), 1),
    ]
    out_specs = [pl.BlockSpec((tm, PACK_WIDTH), lambda i: (i, 0)), pl.BlockSpec((tm, LANES), lambda i: (i, 0))]
    out_shape = [jax.ShapeDtypeStruct((rows, PACK_WIDTH), BF16), jax.ShapeDtypeStruct((rows, LANES), F32)]
    scratch = [pltpu.VMEM((SUBLANES, CONV_DIM), F32),
               pltpu.VMEM((2, SUBLANES + tm, PROJ_CHUNK), F32)]
    if emit_tail:
        assert rows == tm
        out_specs.append(pl.BlockSpec((SUBLANES, CONV_DIM), lambda i: (0, 0)))
        out_shape.append(jax.ShapeDtypeStruct((SUBLANES, CONV_DIM), F32))
    if attention is not None:
        operands += list(attention)
        in_specs += [pl.BlockSpec(memory_space=pltpu.SMEM),
                     _const_spec((ATT_KV_HEADS, N_META, LANES), 1),
                     _const_spec((ATT_KV_HEADS, LANES, N_META), 1)]
        out_specs.append(pl.BlockSpec((tm, ATT_WIDTH), lambda i: (i, 0)))
        out_shape.append(jax.ShapeDtypeStruct((rows, ATT_WIDTH), BF16))
        kv_slots = (tm // BLOCK + 1, ATT_KV_HEADS, BLOCK, LANES)
        scratch += [pltpu.VMEM(kv_slots, BF16), pltpu.VMEM(kv_slots, BF16)]
    return pl.pallas_call(
        functools.partial(_inproj_kernel, tiles_per_seq=tiles_per_seq, first_valid_row=first_valid_row,
                          emit_tail=emit_tail, with_attention=attention is not None),
        grid=(rows // tm,),
        in_specs=in_specs,
        out_specs=out_specs,
        out_shape=out_shape,
        scratch_shapes=scratch,
        compiler_params=pltpu.CompilerParams(dimension_semantics=("arbitrary",),
                                             vmem_limit_bytes=VMEM_LIMIT_IN_PROJ),
        name="in_proj",
    )(*operands)


def _meta_kv(proj_meta):
    k_off, v_off = K_BLK * KV_WIDTH, V_BLK * KV_WIDTH
    k_dup, vt_dup = [], []
    for g in range(ATT_KV_HEADS):
        kg = proj_meta[PAD:, k_off + g * ATT_HEAD_DIM:k_off + (g + 1) * ATT_HEAD_DIM]
        vg = proj_meta[PAD:, v_off + g * ATT_HEAD_DIM:v_off + (g + 1) * ATT_HEAD_DIM]
        k_dup.append(jnp.concatenate([kg, kg], axis=1))
        vt_dup.append(jnp.concatenate([vg.T, vg.T], axis=0))
    return jnp.stack(k_dup), jnp.stack(vt_dup)


def _split3(a):
    hi = a.astype(BF16).astype(F32)
    r1 = a - hi
    mid = r1.astype(BF16).astype(F32)
    return hi, mid, r1 - mid


def _expand_lhs(x):
    hi, mid, lo = _split3(x)
    return (hi + pltpu.roll(mid, SSM_HEADS, axis=1) + pltpu.roll(lo, 2 * SSM_HEADS, axis=1)).astype(BF16)


def _expand_matrix():
    r = np.arange(LANES)[:, None]
    c = np.arange(SSM_INNER)[None, :]
    return jnp.asarray((r < 3 * SSM_HEADS) & (r % SSM_HEADS == c // SSM_HEAD_DIM), BF16)


def _ssd_chunk_fns(xact_ref, dt_ref, dtb_ref, alog_ref, dskip_ref, expand_ref, state_scr, q2row_scr, y_ref):
    emit_y = y_ref is not None
    row = lax.broadcasted_iota(jnp.int32, (BLOCK, BLOCK), 0)
    col = lax.broadcasted_iota(jnp.int32, (BLOCK, BLOCK), 1)
    tril = row >= col
    tril_b = tril.astype(BF16)
    lane = lax.broadcasted_iota(jnp.int32, (1, LANES), 1)
    head_lane = lane < SSM_HEADS
    low_b = (lane < SSM_HEAD_DIM).astype(BF16)
    high_b = 1.0 - low_b
    neg_a2 = -jnp.exp(alog_ref[...]) * LOG2E
    nt = (((1,), (1,)), ((), ()))
    n_bc = SSM_GROUPS * SSM_STATE

    def decays(ch):
        rows = slice(ch * BLOCK, (ch + 1) * BLOCK)
        x_dt = dt_ref[rows, :] + dtb_ref[...]
        dt = jnp.maximum(x_dt, 0.0) + jnp.log1p(jnp.exp(-jnp.abs(x_dt)))
        a2 = dt * neg_a2
        cs3 = jnp.dot(tril_b, jnp.concatenate(_split3(a2), axis=1).astype(BF16), preferred_element_type=F32)
        cs2 = cs3[:, :LANES] + cs3[:, LANES:2 * LANES] + cs3[:, 2 * LANES:]
        q2 = cs2 - jnp.log(dt) * LOG2E
        cs2_last = cs2[BLOCK - 1:BLOCK, :]
        q2row_scr[ch] = q2.T
        w = jnp.where(head_lane, jnp.exp2(cs2_last - q2), 0.0)
        cd = jnp.where(head_lane, jnp.exp2(cs2_last), 0.0)
        parts = [_expand_lhs(w), _expand_lhs(jnp.broadcast_to(cd, (2 * SUBLANES, LANES)))]
        if emit_y:
            parts.insert(0, _expand_lhs(jnp.where(head_lane, jnp.exp2(cs2), 0.0)))
        ex = jnp.dot(jnp.concatenate(parts, axis=0), expand_ref[...], preferred_element_type=F32)
        off = BLOCK if emit_y else 0
        e_exp = ex[:BLOCK] if emit_y else None
        return cs2, e_exp, ex[off:off + BLOCK], ex[off + BLOCK:off + BLOCK + 1]

    def group(ch, g, cs2, e_exp, w_exp, cd_exp):
        rows = slice(ch * BLOCK, (ch + 1) * BLOCK)
        gcols = slice(g * SSM_GROUP_WIDTH, (g + 1) * SSM_GROUP_WIDTH)
        bg = xact_ref[rows,SSM_INNER + g * SSM_STATE:SSM_INNER + (g + 1) * SSM_STATE]
        cg = xact_ref[rows,SSM_INNER + n_bc + g * SSM_STATE:SSM_INNER + n_bc + (g + 1) * SSM_STATE]
        xs_f = xact_ref[rows,gcols].astype(F32)
        s_prev = state_scr[:, gcols]
        bg_t = bg.astype(F32).T.astype(BF16)
        new_states = jnp.dot(bg_t, (xs_f * w_exp[:, gcols]).astype(BF16), preferred_element_type=F32)
        state_scr[:, gcols] = s_prev * cd_exp[:, gcols] + new_states
        if not emit_y:
            return
        cb = lax.dot_general(cg, bg, nt, preferred_element_type=F32)
        y_off = jnp.dot(cg, s_prev.astype(BF16), preferred_element_type=F32)
        for pair in range(SSM_HEADS_PER_GROUP // 2):
            h0 = g * SSM_HEADS_PER_GROUP + 2 * pair
            cols = slice(h0 * SSM_HEAD_DIM, (h0 + 2) * SSM_HEAD_DIM)
            pcols = slice(2 * pair * SSM_HEAD_DIM, (2 * pair + 2) * SSM_HEAD_DIM)
            m_parts = []
            for h in (h0, h0 + 1):
                expo = jnp.where(tril, cs2[:, h:h + 1] - q2row_scr[ch, h:h + 1, :], -jnp.inf)
                m_parts.append((cb * jnp.exp2(expo)).astype(BF16))
            x_pair = xact_ref[rows,cols]
            x_bd = jnp.concatenate([x_pair * low_b, x_pair * high_b], axis=0)
            y_pair = jnp.dot(jnp.concatenate(m_parts, axis=1), x_bd, preferred_element_type=F32)
            y_pair = y_pair + e_exp[:, cols] * y_off[:, pcols] + dskip_ref[:, cols] * xs_f[:, pcols]
            y_ref[rows, cols] = y_pair.astype(y_ref.dtype)

    return decays, group


def _ssd_state_kernel(xact_ref, dt_ref, dtb_ref, alog_ref, dskip_ref, expand_ref, sout_ref, q2row_scr):
    sout_ref[...] = jnp.zeros(sout_ref.shape, F32)
    decays, group = _ssd_chunk_fns(xact_ref, dt_ref, dtb_ref, alog_ref, dskip_ref, expand_ref, sout_ref,
                                   q2row_scr, None)
    terms = decays(0)
    for g in range(SSM_GROUPS):
        group(0, g, *terms)


def _ssd_leading_state(proj_m, dt_m, dt_bias, a_log, d_skip):
    return pl.pallas_call(
        _ssd_state_kernel,
        grid=(1,),
        in_specs=[
            pl.BlockSpec((BLOCK, CONV_DIM), lambda i: (0, XBC_BLK)),
            _const_spec((BLOCK, LANES), 1),
            _const_spec((1, LANES), 1),
            _const_spec((1, LANES), 1),
            _const_spec((1, SSM_INNER), 1),
            _const_spec((LANES, SSM_INNER), 1),
        ],
        out_specs=pl.BlockSpec((SSM_STATE, SSM_INNER), lambda i: (0, 0)),
        out_shape=jax.ShapeDtypeStruct((SSM_STATE, SSM_INNER), F32),
        scratch_shapes=[pltpu.VMEM((1, LANES, BLOCK), F32)],
        compiler_params=pltpu.CompilerParams(vmem_limit_bytes=VMEM_LIMIT),
        name="ssd_leading_state",
    )(proj_m, dt_m, dt_bias, a_log, d_skip, _expand_matrix())


def _ssd_merge_kernel(xact_ref, dt_ref, zssm_ref, dtb_ref, alog_ref, dskip_ref, expand_ref, s0_ref, gnorm_ref,
                      o_ref, zatt_ref, gate_ref, x_ref, woa_ref, wos_ref, wo_ref, gpost_ref,
                      out_ref, state_scr, q2row_scr, y_scr, ssm_scr, att_scr, ya_scr, merged_scr,
                      *, tiles_per_seq, n_tiles):
    tm = x_ref.shape[0]
    i = pl.program_id(0)
    scan_tile = jnp.minimum(i, n_tiles - 1)
    scan_slot = i % 2
    merge_slot = (i + 1) % 2

    @pl.when(scan_tile % tiles_per_seq == 0)
    def _():
        state_scr[...] = s0_ref[...]

    @pl.when(i == 0)
    def _():
        ssm_scr[1] = jnp.zeros(ssm_scr.shape[1:], BF16)

    decays, group = _ssd_chunk_fns(xact_ref, dt_ref, dtb_ref, alog_ref, dskip_ref, expand_ref, state_scr,
                                   q2row_scr, y_scr)

    def branch_piece(ch):
        rows = slice(ch * BLOCK, (ch + 1) * BLOCK)
        y = y_scr[rows, :] * zssm_ref[rows, :].astype(F32)
        for g in range(SSM_GROUPS):
            cols = slice(g * SSM_GROUP_WIDTH, (g + 1) * SSM_GROUP_WIDTH)
            yg = y[:, cols]
            yg = yg * lax.rsqrt(jnp.mean(yg * yg, axis=-1, keepdims=True) + NORM_EPS)
            ssm_scr[scan_slot, rows, cols] = (yg * gnorm_ref[:, cols]).astype(BF16)

    def att_piece(r0):
        rows = slice(r0, r0 + PIECE_ROWS)
        att_scr[rows, :] = (o_ref[rows, :].astype(F32) * zatt_ref[rows, :].astype(F32)).astype(BF16)

    def att_chunk(c0):
        cols = slice(c0, c0 + OUT_CHUNK)
        ya_scr[:, cols] = jnp.dot(att_scr[...], woa_ref[:, cols], preferred_element_type=F32)

    def ssm_merge_chunk(c0):
        cols = slice(c0, c0 + OUT_CHUNK)
        ys = jnp.dot(ssm_scr[merge_slot], wos_ref[:, cols], preferred_element_type=F32)
        gate_att = gate_ref[:, cols].astype(F32)
        gate_ssm = gate_ref[:, D_MODEL + c0:D_MODEL + c0 + OUT_CHUNK].astype(F32)
        merged_scr[:, cols] = (gate_att * ya_scr[:, cols] + gate_ssm * ys).astype(BF16)

    def out_chunk(c0):
        cols = slice(c0, c0 + OUT_CHUNK)
        out_ref[:, cols] = jnp.dot(merged_scr[...], wo_ref[:, cols], preferred_element_type=F32)

    def finish():
        out = out_ref[...]
        ms = jnp.mean(out * out, axis=-1, keepdims=True)
        out_ref[...] = x_ref[...] + out * lax.rsqrt(ms + NORM_EPS) * gpost_ref[...]

    chunks = range(0, D_MODEL, OUT_CHUNK)
    pieces = [functools.partial(att_piece, r0) for r0 in range(0, tm, PIECE_ROWS)]
    matmuls = ([functools.partial(att_chunk, c0) for c0 in chunks]
               + [functools.partial(ssm_merge_chunk, c0) for c0 in chunks]
               + [functools.partial(out_chunk, c0) for c0 in chunks] + [finish])

    n_chunks = tm // BLOCK
    n_slots = n_chunks * SSM_GROUPS
    lead = 2
    per_lead = -(-len(pieces) // lead)
    per_slot = -(-len(matmuls) // (n_slots - lead))
    terms = decays(0)
    slot = 0
    for ch in range(n_chunks):
        next_terms = None
        for g in range(SSM_GROUPS):
            group(ch, g, *terms)
            if g == SSM_GROUPS // 2 - 1 and ch + 1 < n_chunks:
                next_terms = decays(ch + 1)
            if slot < lead:
                todo, pieces = pieces[:per_lead], pieces[per_lead:]
            else:
                todo, matmuls = matmuls[:per_slot], matmuls[per_slot:]
            for piece in todo:
                piece()
            slot += 1
        branch_piece(ch)
        terms = next_terms
    for piece in pieces + matmuls:
        piece()


def _ssd_merge(proj, dt, o_att, x2, dt_bias, a_log, d_skip, s0, g_norm, w_out_att, w_out_ssm, w_out, g_post,
               *, tm, tiles_per_seq):
    rows = x2.shape[0]
    n_tiles = rows // tm
    scan = lambda col: (lambda i: (jnp.minimum(i, n_tiles - 1), col))
    merge = lambda col: (lambda i: (jnp.maximum(i - 1, 0), col))
    return pl.pallas_call(
        functools.partial(_ssd_merge_kernel, tiles_per_seq=tiles_per_seq, n_tiles=n_tiles),
        grid=(n_tiles + 1,),
        in_specs=[
            pl.BlockSpec((tm, CONV_DIM), scan(XBC_BLK)),
            pl.BlockSpec((tm, LANES), scan(0)),
            pl.BlockSpec((tm, SSM_INNER), scan(ZSSM_BLK)),
            _const_spec((1, LANES), 1),
            _const_spec((1, LANES), 1),
            _const_spec((1, SSM_INNER), 1),
            _const_spec((LANES, SSM_INNER), 1),
            _const_spec((SSM_STATE, SSM_INNER), 1),
            _const_spec((1, SSM_INNER), 1),
            pl.BlockSpec((tm, ATT_WIDTH), merge(0)),
            pl.BlockSpec((tm, ATT_WIDTH), merge(ZATT_BLK)),
            pl.BlockSpec((tm, 2 * D_MODEL), merge(GATE_BLK)),
            pl.BlockSpec((tm, D_MODEL), merge(0)),
            _const_spec((ATT_WIDTH, D_MODEL), 1),
            _const_spec((SSM_INNER, D_MODEL), 1),
            _const_spec((D_MODEL, D_MODEL), 1),
            _const_spec((1, D_MODEL), 1),
        ],
        out_specs=pl.BlockSpec((tm, D_MODEL), merge(0)),
        out_shape=jax.ShapeDtypeStruct((rows, D_MODEL), F32),
        scratch_shapes=[
            pltpu.VMEM((SSM_STATE, SSM_INNER), F32),
            pltpu.VMEM((tm // BLOCK, LANES, BLOCK), F32),
            pltpu.VMEM((tm, SSM_INNER), F32),
            pltpu.VMEM((2, tm, SSM_INNER), BF16),
            pltpu.VMEM((tm, ATT_WIDTH), BF16),
            pltpu.VMEM((tm, D_MODEL), F32),
            pltpu.VMEM((tm, D_MODEL), BF16),
        ],
        compiler_params=pltpu.CompilerParams(dimension_semantics=("arbitrary",),
                                             vmem_limit_bytes=VMEM_LIMIT_SSD_MERGE),
        name="ssd_merge",
    )(proj, dt, proj, dt_bias, a_log, d_skip, _expand_matrix(), s0, g_norm,
      o_att, proj, proj, x2, w_out_att, w_out_ssm, w_out, g_post)


def _transpose_cast_kernel(wt_ref, o_ref):
    o_ref[...] = wt_ref[...].T.astype(o_ref.dtype)


def _transpose_cast(w_t, n_features, *, block):
    assert n_features % block == 0 and block % LANES == 0
    return pl.pallas_call(
        _transpose_cast_kernel,
        grid=(n_features // block,),
        in_specs=[pl.BlockSpec((block, D_MODEL), lambda j: (j, 0))],
        out_specs=pl.BlockSpec((D_MODEL, block), lambda j: (0, j)),
        out_shape=jax.ShapeDtypeStruct((D_MODEL, n_features), BF16),
        compiler_params=pltpu.CompilerParams(vmem_limit_bytes=VMEM_LIMIT),
        name="transpose_cast_weight",
    )(w_t)


def _in_proj_weights(w_in):
    assert w_in.shape[1] == W_GATE_OFF + 2 * D_MODEL == sum(SPLIT_SIZES)
    w_t = w_in.T
    w_main = _transpose_cast(w_t, W_MAIN_WIDTH, block=W_MAIN_WIDTH // 5)
    n_tail = SSM_HEADS + 2 * D_MODEL
    w_gate, w_dt = pl.pallas_call(
        _tail_weights_kernel,
        grid=(1,),
        in_specs=[pl.BlockSpec((pl.Element(n_tail), pl.Element(D_MODEL)), lambda j: (W_DT_OFF, 0))],
        out_specs=[pl.BlockSpec((D_MODEL, 2 * D_MODEL), lambda j: (0, 0)),
                   pl.BlockSpec((D_MODEL, LANES), lambda j: (0, 0))],
        out_shape=[jax.ShapeDtypeStruct((D_MODEL, 2 * D_MODEL), BF16),
                   jax.ShapeDtypeStruct((D_MODEL, LANES), BF16)],
        compiler_params=pltpu.CompilerParams(vmem_limit_bytes=VMEM_LIMIT),
        name="transpose_cast_tail",
    )(w_t)
    return w_main, w_gate, w_dt


def _tail_weights_kernel(wt_ref, gate_ref, dt_ref):
    gate_ref[...] = wt_ref[SSM_HEADS:, :].T.astype(BF16)
    dt_cols = wt_ref[:LANES, :].T
    lane = lax.broadcasted_iota(jnp.int32, (1, LANES), 1)
    dt_ref[...] = jnp.where(lane < SSM_HEADS, dt_cols, 0.0).astype(BF16)


def _pad_lanes(v):
    return jnp.pad(v.astype(F32), (0, LANES - v.shape[0])).reshape(1, LANES)


def kernel(x, meta_tokens, g_pre, w_in, conv_w, conv_b, dt_bias, a_log, d_skip, attn_sinks, g_ssm_norm,
           w_out_att, w_out_ssm, w_out, g_post):
    b, seq, d = x.shape
    assert d == D_MODEL and PAD + N_META == BLOCK
    assert g_pre.shape[0] == 1, "one layer"
    in_tile, out_tile = 512, 512
    assert seq % in_tile == 0 and seq % out_tile == 0

    in_weights = _in_proj_weights(w_in[0])
    g_pre2 = g_pre[0].reshape(1, D_MODEL)
    conv_b2 = conv_b[0].reshape(1, CONV_DIM)
    ssd_params = (_pad_lanes(dt_bias[0]), _pad_lanes(a_log[0]),
                  jnp.repeat(d_skip[0].astype(F32), SSM_HEAD_DIM).reshape(1, SSM_INNER))

    h_meta = jnp.concatenate([jnp.zeros((PAD, D_MODEL), x.dtype), meta_tokens.astype(x.dtype)], axis=0)
    proj_m, dt_m, tail_m = _inproj(h_meta, g_pre2, *in_weights, conv_w[0], conv_b2,
                                   jnp.zeros((SUBLANES, CONV_DIM), F32),
                                   tm=BLOCK, tiles_per_seq=1, first_valid_row=PAD, emit_tail=True)
    state_m = _ssd_leading_state(proj_m, dt_m, *ssd_params)

    x2 = x.reshape(b * seq, D_MODEL)
    proj, dt, o_att = _inproj(x2, g_pre2, *in_weights, conv_w[0], conv_b2, tail_m,
                              (attn_sinks[0].astype(F32), *_meta_kv(proj_m)),
                              tm=in_tile, tiles_per_seq=seq // in_tile, first_valid_row=0, emit_tail=False)
    out = _ssd_merge(proj, dt, o_att, x2, *ssd_params, state_m, g_ssm_norm[0].reshape(1, SSM_INNER),
                     w_out_att[0].astype(BF16), w_out_ssm[0].astype(BF16), w_out[0].astype(BF16),
                     g_post[0].reshape(1, D_MODEL), tm=out_tile, tiles_per_seq=seq // out_tile)
    return out.reshape(b, seq, D_MODEL)
```

```python
import functools
import math

import numpy as np
import jax
import jax.numpy as jnp
from jax import lax
from jax.experimental import pallas as pl
from jax.experimental.pallas import tpu as pltpu

F32 = jnp.float32
BF16 = jnp.bfloat16

D_MODEL = 1024
N_META = 16
BLOCK = 128
PAD = (-N_META) % BLOCK
NORM_EPS = 1e-6

ATT_HEAD_DIM = 64
ATT_Q_HEADS = D_MODEL // ATT_HEAD_DIM
ATT_KV_HEADS = 4
ATT_GROUP = ATT_Q_HEADS // ATT_KV_HEADS
ATT_WIDTH = ATT_Q_HEADS * ATT_HEAD_DIM
KV_WIDTH = ATT_KV_HEADS * ATT_HEAD_DIM

SSM_INNER = 2 * D_MODEL
SSM_HEAD_DIM = 64
SSM_HEADS = SSM_INNER // SSM_HEAD_DIM
SSM_GROUPS = 4
SSM_HEADS_PER_GROUP = SSM_HEADS // SSM_GROUPS
SSM_GROUP_WIDTH = SSM_INNER // SSM_GROUPS
SSM_STATE = 128
CONV_WIDTH = 4
CONV_DIM = SSM_INNER + 2 * SSM_GROUPS * SSM_STATE

SPLIT_SIZES = (ATT_WIDTH, KV_WIDTH, KV_WIDTH, ATT_WIDTH, SSM_INNER, CONV_DIM, SSM_HEADS, D_MODEL, D_MODEL)

LANES = 128
SUBLANES = 8
LOG2E = math.log2(math.e)

PACK_WIDTH = ATT_WIDTH + ATT_WIDTH + SSM_INNER + 2 * D_MODEL + CONV_DIM + 2 * KV_WIDTH
Q_BLK = 0
ZATT_BLK = 1
ZSSM_BLK = 1
GATE_BLK = 2
XBC_BLK = 2
XBC_OFF = XBC_BLK * CONV_DIM
K_BLK = (PACK_WIDTH - 2 * KV_WIDTH) // KV_WIDTH
V_BLK = K_BLK + 1
PROJ_CHUNK = 256
CONV_ROWS = 64
OUT_CHUNK = 256
PIECE_ROWS = 64
W_Q_OFF = 0
W_KV_OFF = ATT_WIDTH
W_ZATT_OFF = W_KV_OFF + 2 * KV_WIDTH
W_ZSSM_OFF = W_ZATT_OFF + ATT_WIDTH
W_XBC_OFF = W_ZSSM_OFF + SSM_INNER
W_MAIN_WIDTH = W_XBC_OFF + CONV_DIM
W_DT_OFF = W_MAIN_WIDTH
W_GATE_OFF = W_DT_OFF + SSM_HEADS
VMEM_LIMIT = 48 * 1024 * 1024
VMEM_LIMIT_IN_PROJ = 58 * 1024 * 1024
VMEM_LIMIT_SSD_MERGE = 56 * 1024 * 1024


def _sigmoid(x):
    return 0.5 * jnp.tanh(0.5 * x) + 0.5


def _silu(x):
    half = 0.5 * x
    return half * jnp.tanh(half) + half


def _const_spec(shape, n_grid):
    zeros = (0,) * len(shape)
    index_map = (lambda i: zeros) if n_grid == 1 else (lambda i, j: zeros)
    return pl.BlockSpec(shape, index_map, pipeline_mode=pl.Buffered(1))


def _inproj_kernel(h_ref, g_ref, w_ref, wgate_ref, wdt_ref, convw_ref, convb_ref, tail0_ref, *rest,
                   tiles_per_seq, first_valid_row, emit_tail, with_attention):
    rest = list(rest)
    if with_attention:
        sink_ref, km_ref, vmt_ref = rest[:3]
        del rest[:3]
    proj_ref, dt_ref = rest[:2]
    del rest[:2]
    tailout_ref = rest.pop(0) if emit_tail else None
    o_ref = rest.pop(0) if with_attention else None
    tail_scr, raw_scr = rest[:2]
    if with_attention:
        kd_scr, vtd_scr = rest[2:]
    tm = h_ref.shape[0]
    i = pl.program_id(0)
    first_tile = i % tiles_per_seq == 0

    @pl.when(first_tile)
    def _():
        tail_scr[...] = tail0_ref[...]

    h = h_ref[...]
    ms = jnp.mean(h * h, axis=-1, keepdims=True)
    u = (h * lax.rsqrt(ms + NORM_EPS) * g_ref[...]).astype(BF16)
    dt_ref[...] = jnp.dot(u, wdt_ref[...], preferred_element_type=F32)

    def project(src_ref, c0):
        return jnp.dot(u, src_ref[:, c0:c0 + PROJ_CHUNK], preferred_element_type=F32)

    def plain_chunk(src_ref, src_c0, out_c0):
        r = project(src_ref, src_c0)
        if out_c0 < ZATT_BLK * ATT_WIDTH:
            r = r * (ATT_HEAD_DIM ** -0.5)
        elif out_c0 < GATE_BLK * 2 * D_MODEL:
            r = _silu(r)
        elif out_c0 < XBC_OFF:
            r = _sigmoid(r)
        proj_ref[:, out_c0:out_c0 + PROJ_CHUNK] = r.astype(BF16)

    def xbc_matmul(k):
        cols = slice(k * PROJ_CHUNK, (k + 1) * PROJ_CHUNK)
        raw = raw_scr.at[k % 2]
        raw[0:SUBLANES, :] = tail_scr[:, cols]
        raw[SUBLANES:, :] = project(w_ref, W_XBC_OFF + k * PROJ_CHUNK)
        new_tail = raw[tm:, :]
        tail_scr[:, cols] = new_tail
        if emit_tail:
            tailout_ref[:, cols] = new_tail

    def conv_piece(k, r0):
        raw = raw_scr.at[k % 2]
        for l0 in range(0, PROJ_CHUNK, LANES):
            cols = slice(k * PROJ_CHUNK + l0, k * PROJ_CHUNK + l0 + LANES)
            w0, w1, w2, w3 = (convw_ref[t:t + 1, cols] for t in range(CONV_WIDTH))
            ext = raw[r0:r0 + SUBLANES + CONV_ROWS, l0:l0 + LANES]
            ext1 = pltpu.roll(ext, 1, axis=0)
            pair_old = w1 * ext + w0 * ext1
            acc = (convb_ref[:, cols] + w3 * ext[SUBLANES:] + w2 * ext1[SUBLANES:]
                   + pltpu.roll(pair_old, 2, axis=0)[SUBLANES:])
            xact = _silu(acc)
            if first_valid_row:
                rows = lax.broadcasted_iota(jnp.int32, (CONV_ROWS, 1), 0) + r0
                xact = jnp.where(rows >= first_valid_row, xact, 0.0)
            proj_ref[r0:r0 + CONV_ROWS, XBC_OFF + cols.start:XBC_OFF + cols.stop] = xact.astype(BF16)

    if with_attention:
        n_blocks = tm // BLOCK
        n_lanes = ATT_GROUP * BLOCK
        key = lax.broadcasted_iota(jnp.int32, (BLOCK, n_lanes), 0)
        qry = lax.broadcasted_iota(jnp.int32, (BLOCK, n_lanes), 1) % BLOCK
        in_cur = key <= qry
        rel = jnp.where(in_cur, qry - key, qry - key + BLOCK).astype(F32)
        rel_first = jnp.where(jnp.logical_or(in_cur, jnp.logical_not(first_tile)), rel, jnp.inf)
        cur_b = in_cur.astype(BF16)
        prev_b = 1.0 - cur_b
        low_lanes = lax.broadcasted_iota(jnp.int32, (1, LANES), 1) < ATT_HEAD_DIM
        nt = (((1,), (1,)), ((), ()))

        @pl.when(first_tile)
        def _():
            kd_scr[0] = jnp.zeros(kd_scr.shape[1:], BF16)
            vtd_scr[0] = jnp.zeros(vtd_scr.shape[1:], BF16)

        @pl.when(jnp.logical_not(first_tile))
        def _():
            kd_scr[0] = kd_scr[n_blocks]
            vtd_scr[0] = vtd_scr[n_blocks]

    def arrange_kv(blk):
        rows = slice(blk * BLOCK, (blk + 1) * BLOCK)
        k_f = proj_ref[rows, K_BLK * KV_WIDTH:(K_BLK + 1) * KV_WIDTH].astype(F32)
        v_f = proj_ref[rows, V_BLK * KV_WIDTH:(V_BLK + 1) * KV_WIDTH].astype(F32)
        for t in range(KV_WIDTH // LANES):
            k_tile = k_f[:, t * LANES:(t + 1) * LANES]
            k_swap = pltpu.roll(k_tile, ATT_HEAD_DIM, axis=1)
            kd_scr[blk + 1, 2 * t] = jnp.where(low_lanes, k_tile, k_swap).astype(BF16)
            kd_scr[blk + 1, 2 * t + 1] = jnp.where(low_lanes, k_swap, k_tile).astype(BF16)
            vt = v_f[:, t * LANES:(t + 1) * LANES].T
            vtd_scr[blk + 1, 2 * t] = vt[:ATT_HEAD_DIM].astype(BF16)
            vtd_scr[blk + 1, 2 * t + 1] = vt[ATT_HEAD_DIM:].astype(BF16)

    def attend(blk, g):
        rows = slice(blk * BLOCK, (blk + 1) * BLOCK)
        k_dup = jnp.concatenate([kd_scr[blk, g], kd_scr[blk + 1, g]], axis=0)
        v_t = jnp.concatenate([vtd_scr[blk, g], vtd_scr[blk + 1, g]], axis=1)
        q_rows = []
        for t in (2 * g, 2 * g + 1):
            q_tile = proj_ref[rows, Q_BLK * ATT_WIDTH + t * LANES:Q_BLK * ATT_WIDTH + (t + 1) * LANES]
            q_rows += [jnp.where(low_lanes, q_tile, 0), jnp.where(low_lanes, 0, q_tile)]
        q_cat = jnp.concatenate(q_rows, axis=0)
        s_band = lax.dot_general(k_dup, q_cat, nt, preferred_element_type=F32)
        s_meta = lax.dot_general(km_ref[g], q_cat, nt, preferred_element_type=F32)

        heads = range(g * ATT_GROUP, (g + 1) * ATT_GROUP)
        slopes = jnp.concatenate(
            [jnp.full((1, BLOCK), 2.0 ** (-8.0 * (h + 1) / ATT_Q_HEADS), F32) for h in heads], axis=1)
        sinks = jnp.concatenate([jnp.full((1, BLOCK), sink_ref[h], F32) for h in heads], axis=1)
        s = jnp.where(in_cur, s_band[BLOCK:], s_band[:BLOCK]) - slopes * (rel_first if blk == 0 else rel)
        m = jnp.maximum(jnp.max(s, axis=0, keepdims=True), jnp.max(s_meta, axis=0, keepdims=True))
        m = jnp.maximum(m, sinks)
        p = jnp.exp(s - m)
        p_meta = jnp.exp(s_meta - m)
        denom = (jnp.sum(p, axis=0, keepdims=True) + jnp.sum(p_meta, axis=0, keepdims=True)
                 + jnp.exp(sinks - m))
        p_b = p.astype(BF16)
        p_band = jnp.concatenate([p_b * prev_b, p_b * cur_b], axis=0)
        o_t = jnp.dot(v_t, p_band, preferred_element_type=F32)
        o_t = o_t + jnp.dot(vmt_ref[g], p_meta.astype(BF16), preferred_element_type=F32)
        o_t = o_t * (1.0 / denom)
        for pair in range(ATT_GROUP // 2):
            head_a = o_t[:, (2 * pair) * BLOCK:(2 * pair + 1) * BLOCK]
            head_b = o_t[:, (2 * pair + 1) * BLOCK:(2 * pair + 2) * BLOCK]
            tile = 2 * g + pair
            both = jnp.concatenate([head_a, head_b], axis=0)
            o_ref[rows, tile * LANES:(tile + 1) * LANES] = both.T.astype(BF16)

    early, late = _plain_chunks(w_ref, wgate_ref)
    for chunk in early:
        plain_chunk(*chunk)
    attn_pending = []
    if with_attention:
        for blk in range(n_blocks):
            arrange_kv(blk)
        attn_pending = [(blk, g) for blk in range(n_blocks) for g in range(ATT_KV_HEADS)]
    n_xbc = CONV_DIM // PROJ_CHUNK
    n_items = n_xbc + len(late)
    xbc_at = {k * n_items // n_xbc: k for k in range(n_xbc)}
    late_iter = iter(late)
    matmuls = [xbc_at[n] if n in xbc_at else next(late_iter) for n in range(n_items)]
    conv_per_matmul = -(-n_xbc * (tm // CONV_ROWS) // (len(matmuls) - 1))
    attn_per_matmul = -(-len(attn_pending) // len(matmuls))
    conv_pending = []
    for item in matmuls:
        if isinstance(item, int):
            assert all(k > item - 2 for k, _ in conv_pending), "raw_scr slot still in use"
            xbc_matmul(item)
            conv_pending += [(item, r0) for r0 in range(0, tm, CONV_ROWS)]
        else:
            plain_chunk(*item)
        for piece in conv_pending[:conv_per_matmul]:
            conv_piece(*piece)
        conv_pending = conv_pending[conv_per_matmul:]
        for piece in attn_pending[:attn_per_matmul]:
            attend(*piece)
        attn_pending = attn_pending[attn_per_matmul:]
    for piece in conv_pending:
        conv_piece(*piece)
    for piece in attn_pending:
        attend(*piece)


def _plain_chunks(w_ref, wgate_ref):
    def chunks(ref, src, dst, width):
        return [(ref, src + c, dst + c) for c in range(0, width, PROJ_CHUNK)]

    early = chunks(w_ref, W_Q_OFF, Q_BLK * ATT_WIDTH, ATT_WIDTH) + chunks(w_ref, W_KV_OFF, K_BLK * KV_WIDTH, 2 * KV_WIDTH)
    late = (chunks(w_ref, W_ZATT_OFF, ZATT_BLK * ATT_WIDTH, ATT_WIDTH)
            + chunks(w_ref, W_ZSSM_OFF, ZSSM_BLK * SSM_INNER, SSM_INNER)
            + chunks(wgate_ref, 0, GATE_BLK * 2 * D_MODEL, 2 * D_MODEL))
    return early, late


def _inproj(h2, g_pre, w_main, w_gate, w_dt, conv_w, conv_b, tail0, attention=None, *, tm, tiles_per_seq,
            first_valid_row, emit_tail):
    rows = h2.shape[0]
    assert tm % CONV_ROWS == 0 and tm % BLOCK == 0
    operands = [h2, g_pre, w_main, w_gate, w_dt, conv_w, conv_b, tail0]
    in_specs = [
        pl.BlockSpec((tm, D_MODEL), lambda i: (i, 0)),
        _const_spec((1, D_MODEL), 1),
        _const_spec((D_MODEL, W_MAIN_WIDTH), 1),
        _const_spec((D_MODEL, 2 * D_MODEL), 1),
        _const_spec((D_MODEL, LANES), 1),
        _const_spec((CONV_WIDTH, CONV_DIM), 1),
        _const_spec((1, CONV_DIM), 1),
        _const_spec((SUBLANES, CONV_DIM), 1),
    ]
    out_specs = [pl.BlockSpec((tm, PACK_WIDTH), lambda i: (i, 0)), pl.BlockSpec((tm, LANES), lambda i: (i, 0))]
    out_shape = [jax.ShapeDtypeStruct((rows, PACK_WIDTH), BF16), jax.ShapeDtypeStruct((rows, LANES), F32)]
    scratch = [pltpu.VMEM((SUBLANES, CONV_DIM), F32),
               pltpu.VMEM((2, SUBLANES + tm, PROJ_CHUNK), F32)]
    if emit_tail:
        assert rows == tm
        out_specs.append(pl.BlockSpec((SUBLANES, CONV_DIM), lambda i: (0, 0)))
        out_shape.append(jax.ShapeDtypeStruct((SUBLANES, CONV_DIM), F32))
    if attention is not None:
        operands += list(attention)
        in_specs += [pl.BlockSpec(memory_space=pltpu.SMEM),
                     _const_spec((ATT_KV_HEADS, N_META, LANES), 1),
                     _const_spec((ATT_KV_HEADS, ATT_HEAD_DIM, N_META), 1)]
        out_specs.append(pl.BlockSpec((tm, ATT_WIDTH), lambda i: (i, 0)))
        out_shape.append(jax.ShapeDtypeStruct((rows, ATT_WIDTH), BF16))
        kv_slots = (tm // BLOCK + 1, ATT_KV_HEADS)
        scratch += [pltpu.VMEM(kv_slots + (BLOCK, LANES), BF16),
                    pltpu.VMEM(kv_slots + (ATT_HEAD_DIM, BLOCK), BF16)]
    return pl.pallas_call(
        functools.partial(_inproj_kernel, tiles_per_seq=tiles_per_seq, first_valid_row=first_valid_row,
                          emit_tail=emit_tail, with_attention=attention is not None),
        grid=(rows // tm,),
        in_specs=in_specs,
        out_specs=out_specs,
        out_shape=out_shape,
        scratch_shapes=scratch,
        compiler_params=pltpu.CompilerParams(dimension_semantics=("arbitrary",),
                                             vmem_limit_bytes=VMEM_LIMIT_IN_PROJ),
        name="in_proj",
    )(*operands)


def _meta_kv(proj_meta):
    k_off, v_off = K_BLK * KV_WIDTH, V_BLK * KV_WIDTH
    k_dup, vt_dup = [], []
    for g in range(ATT_KV_HEADS):
        kg = proj_meta[PAD:, k_off + g * ATT_HEAD_DIM:k_off + (g + 1) * ATT_HEAD_DIM]
        vg = proj_meta[PAD:, v_off + g * ATT_HEAD_DIM:v_off + (g + 1) * ATT_HEAD_DIM]
        k_dup.append(jnp.concatenate([kg, kg], axis=1))
        vt_dup.append(vg.T)
    return jnp.stack(k_dup), jnp.stack(vt_dup)


def _split3(a):
    hi = a.astype(BF16).astype(F32)
    r1 = a - hi
    mid = r1.astype(BF16).astype(F32)
    return hi, mid, r1 - mid


def _expand_lhs(x):
    hi, mid, lo = _split3(x)
    return (hi + pltpu.roll(mid, SSM_HEADS, axis=1) + pltpu.roll(lo, 2 * SSM_HEADS, axis=1)).astype(BF16)


def _expand_matrix():
    r = np.arange(LANES)[:, None]
    c = np.arange(SSM_INNER)[None, :]
    return jnp.asarray((r < 3 * SSM_HEADS) & (r % SSM_HEADS == c // SSM_HEAD_DIM), BF16)


def _ssd_chunk_fns(xact_ref, dt_ref, dtb_ref, alog_ref, dskip_ref, expand_ref, state_scr, q2row_scr, y_ref):
    emit_y = y_ref is not None
    row = lax.broadcasted_iota(jnp.int32, (BLOCK, BLOCK), 0)
    col = lax.broadcasted_iota(jnp.int32, (BLOCK, BLOCK), 1)
    tril = row >= col
    tril_b = tril.astype(BF16)
    lane = lax.broadcasted_iota(jnp.int32, (1, LANES), 1)
    head_lane = lane < SSM_HEADS
    low_lane = lane < SSM_HEAD_DIM
    low_b = low_lane.astype(BF16)
    high_b = 1.0 - low_b
    neg_a2 = -jnp.exp(alog_ref[...]) * LOG2E
    nt = (((1,), (1,)), ((), ()))
    n_bc = SSM_GROUPS * SSM_STATE

    def decays(ch):
        rows = slice(ch * BLOCK, (ch + 1) * BLOCK)
        x_dt = dt_ref[rows, :] + dtb_ref[...]
        dt = jnp.maximum(x_dt, 0.0) + jnp.log1p(jnp.exp(-jnp.abs(x_dt)))
        a2 = dt * neg_a2
        cs3 = jnp.dot(tril_b, jnp.concatenate(_split3(a2), axis=1).astype(BF16), preferred_element_type=F32)
        cs2 = cs3[:, :LANES] + cs3[:, LANES:2 * LANES] + cs3[:, 2 * LANES:]
        q2 = cs2 - jnp.log(dt) * LOG2E
        cs2_last = cs2[BLOCK - 1:BLOCK, :]
        q2row_scr[ch] = q2.T
        w = jnp.where(head_lane, jnp.exp2(cs2_last - q2), 0.0)
        cd = jnp.where(head_lane, jnp.exp2(cs2_last), 0.0)
        parts = [_expand_lhs(w), _expand_lhs(jnp.broadcast_to(cd, (2 * SUBLANES, LANES)))]
        ex = jnp.dot(jnp.concatenate(parts, axis=0), expand_ref[...], preferred_element_type=F32)
        return cs2, ex[:BLOCK], ex[BLOCK:BLOCK + 1]

    def group(ch, g, cs2, w_exp, cd_exp):
        rows = slice(ch * BLOCK, (ch + 1) * BLOCK)
        gcols = slice(g * SSM_GROUP_WIDTH, (g + 1) * SSM_GROUP_WIDTH)
        bg = xact_ref[rows,SSM_INNER + g * SSM_STATE:SSM_INNER + (g + 1) * SSM_STATE]
        cg = xact_ref[rows,SSM_INNER + n_bc + g * SSM_STATE:SSM_INNER + n_bc + (g + 1) * SSM_STATE]
        xs_f = xact_ref[rows,gcols].astype(F32)
        s_prev = state_scr[:, gcols]
        bg_t = bg.astype(F32).T.astype(BF16)
        new_states = jnp.dot(bg_t, (xs_f * w_exp[:, gcols]).astype(BF16), preferred_element_type=F32)
        state_scr[:, gcols] = s_prev * cd_exp[:, gcols] + new_states
        if not emit_y:
            return
        cb = lax.dot_general(cg, bg, nt, preferred_element_type=F32)
        y_off = jnp.dot(cg, s_prev.astype(BF16), preferred_element_type=F32)
        for pair in range(SSM_HEADS_PER_GROUP // 2):
            h0 = g * SSM_HEADS_PER_GROUP + 2 * pair
            cols = slice(h0 * SSM_HEAD_DIM, (h0 + 2) * SSM_HEAD_DIM)
            pcols = slice(2 * pair * SSM_HEAD_DIM, (2 * pair + 2) * SSM_HEAD_DIM)
            m_parts, from_start = [], []
            for h in (h0, h0 + 1):
                cs_l = jnp.broadcast_to(cs2[:, h:h + 1], (BLOCK, BLOCK))
                expo = jnp.where(tril, cs_l - q2row_scr[ch, h:h + 1, :], -jnp.inf)
                m_parts.append((cb * jnp.exp2(expo)).astype(BF16))
                from_start.append(jnp.exp2(cs_l))
            e_pair = jnp.where(low_lane, from_start[0], from_start[1])
            x_pair = xact_ref[rows,cols]
            x_bd = jnp.concatenate([x_pair * low_b, x_pair * high_b], axis=0)
            y_pair = jnp.dot(jnp.concatenate(m_parts, axis=1), x_bd, preferred_element_type=F32)
            y_pair = y_pair + e_pair * y_off[:, pcols] + dskip_ref[:, cols] * xs_f[:, pcols]
            y_ref[rows, cols] = y_pair.astype(y_ref.dtype)

    return decays, group


def _ssd_state_kernel(xact_ref, dt_ref, dtb_ref, alog_ref, dskip_ref, expand_ref, sout_ref, q2row_scr):
    sout_ref[...] = jnp.zeros(sout_ref.shape, F32)
    decays, group = _ssd_chunk_fns(xact_ref, dt_ref, dtb_ref, alog_ref, dskip_ref, expand_ref, sout_ref,
                                   q2row_scr, None)
    terms = decays(0)
    for g in range(SSM_GROUPS):
        group(0, g, *terms)


def _ssd_leading_state(proj_m, dt_m, dt_bias, a_log, d_skip):
    return pl.pallas_call(
        _ssd_state_kernel,
        grid=(1,),
        in_specs=[
            pl.BlockSpec((BLOCK, CONV_DIM), lambda i: (0, XBC_BLK)),
            _const_spec((BLOCK, LANES), 1),
            _const_spec((1, LANES), 1),
            _const_spec((1, LANES), 1),
            _const_spec((1, SSM_INNER), 1),
            _const_spec((LANES, SSM_INNER), 1),
        ],
        out_specs=pl.BlockSpec((SSM_STATE, SSM_INNER), lambda i: (0, 0)),
        out_shape=jax.ShapeDtypeStruct((SSM_STATE, SSM_INNER), F32),
        scratch_shapes=[pltpu.VMEM((1, LANES, BLOCK), F32)],
        compiler_params=pltpu.CompilerParams(vmem_limit_bytes=VMEM_LIMIT),
        name="ssd_leading_state",
    )(proj_m, dt_m, dt_bias, a_log, d_skip, _expand_matrix())


def _ssd_merge_kernel(xact_ref, dt_ref, zssm_ref, dtb_ref, alog_ref, dskip_ref, expand_ref, s0_ref, gnorm_ref,
                      o_ref, zatt_ref, gate_ref, x_ref, woa_ref, wos_ref, wo_ref, gpost_ref,
                      out_ref, state_scr, q2row_scr, y_scr, ssm_scr, att_scr, ya_scr, merged_scr,
                      *, tiles_per_seq, n_tiles):
    tm = x_ref.shape[0]
    i = pl.program_id(0)
    scan_tile = jnp.minimum(i, n_tiles - 1)
    scan_slot = i % 2
    merge_slot = (i + 1) % 2

    @pl.when(scan_tile % tiles_per_seq == 0)
    def _():
        state_scr[...] = s0_ref[...]

    @pl.when(i == 0)
    def _():
        ssm_scr[1] = jnp.zeros(ssm_scr.shape[1:], BF16)

    decays, group = _ssd_chunk_fns(xact_ref, dt_ref, dtb_ref, alog_ref, dskip_ref, expand_ref, state_scr,
                                   q2row_scr, y_scr)

    def branch_piece(ch):
        rows = slice(ch * BLOCK, (ch + 1) * BLOCK)
        y = y_scr[rows, :] * zssm_ref[rows, :].astype(F32)
        for g in range(SSM_GROUPS):
            cols = slice(g * SSM_GROUP_WIDTH, (g + 1) * SSM_GROUP_WIDTH)
            yg = y[:, cols]
            yg = yg * lax.rsqrt(jnp.mean(yg * yg, axis=-1, keepdims=True) + NORM_EPS)
            ssm_scr[scan_slot, rows, cols] = (yg * gnorm_ref[:, cols]).astype(BF16)

    def att_piece(r0):
        rows = slice(r0, r0 + PIECE_ROWS)
        att_scr[rows, :] = (o_ref[rows, :].astype(F32) * zatt_ref[rows, :].astype(F32)).astype(BF16)

    def att_chunk(c0):
        cols = slice(c0, c0 + OUT_CHUNK)
        ya_scr[:, cols] = jnp.dot(att_scr[...], woa_ref[:, cols], preferred_element_type=F32)

    def ssm_merge_chunk(c0):
        cols = slice(c0, c0 + OUT_CHUNK)
        ys = jnp.dot(ssm_scr[merge_slot], wos_ref[:, cols], preferred_element_type=F32)
        gate_att = gate_ref[:, cols].astype(F32)
        gate_ssm = gate_ref[:, D_MODEL + c0:D_MODEL + c0 + OUT_CHUNK].astype(F32)
        merged_scr[:, cols] = (gate_att * ya_scr[:, cols] + gate_ssm * ys).astype(BF16)

    def out_chunk(c0):
        cols = slice(c0, c0 + OUT_CHUNK)
        out_ref[:, cols] = jnp.dot(merged_scr[...], wo_ref[:, cols], preferred_element_type=F32)

    def finish():
        out = out_ref[...]
        ms = jnp.mean(out * out, axis=-1, keepdims=True)
        out_ref[...] = x_ref[...] + out * lax.rsqrt(ms + NORM_EPS) * gpost_ref[...]

    chunks = range(0, D_MODEL, OUT_CHUNK)
    pieces = [functools.partial(att_piece, r0) for r0 in range(0, tm, PIECE_ROWS)]
    matmuls = ([functools.partial(att_chunk, c0) for c0 in chunks]
               + [functools.partial(ssm_merge_chunk, c0) for c0 in chunks]
               + [functools.partial(out_chunk, c0) for c0 in chunks] + [finish])

    n_chunks = tm // BLOCK
    n_slots = n_chunks * SSM_GROUPS
    lead = 2
    per_lead = -(-len(pieces) // lead)
    per_slot = -(-len(matmuls) // (n_slots - lead))
    terms = decays(0)
    slot = 0
    for ch in range(n_chunks):
        next_terms = None
        for g in range(SSM_GROUPS):
            group(ch, g, *terms)
            if g == SSM_GROUPS // 2 - 1 and ch + 1 < n_chunks:
                next_terms = decays(ch + 1)
            if slot < lead:
                todo, pieces = pieces[:per_lead], pieces[per_lead:]
            else:
                todo, matmuls = matmuls[:per_slot], matmuls[per_slot:]
            for piece in todo:
                piece()
            slot += 1
        branch_piece(ch)
        terms = next_terms
    for piece in pieces + matmuls:
        piece()


def _ssd_merge(proj, dt, o_att, x2, dt_bias, a_log, d_skip, s0, g_norm, w_out_att, w_out_ssm, w_out, g_post,
               *, tm, tiles_per_seq):
    rows = x2.shape[0]
    n_tiles = rows // tm
    scan = lambda col: (lambda i: (jnp.minimum(i, n_tiles - 1), col))
    merge = lambda col: (lambda i: (jnp.maximum(i - 1, 0), col))
    return pl.pallas_call(
        functools.partial(_ssd_merge_kernel, tiles_per_seq=tiles_per_seq, n_tiles=n_tiles),
        grid=(n_tiles + 1,),
        in_specs=[
            pl.BlockSpec((tm, CONV_DIM), scan(XBC_BLK)),
            pl.BlockSpec((tm, LANES), scan(0)),
            pl.BlockSpec((tm, SSM_INNER), scan(ZSSM_BLK)),
            _const_spec((1, LANES), 1),
            _const_spec((1, LANES), 1),
            _const_spec((1, SSM_INNER), 1),
            _const_spec((LANES, SSM_INNER), 1),
            _const_spec((SSM_STATE, SSM_INNER), 1),
            _const_spec((1, SSM_INNER), 1),
            pl.BlockSpec((tm, ATT_WIDTH), merge(0)),
            pl.BlockSpec((tm, ATT_WIDTH), merge(ZATT_BLK)),
            pl.BlockSpec((tm, 2 * D_MODEL), merge(GATE_BLK)),
            pl.BlockSpec((tm, D_MODEL), merge(0)),
            _const_spec((ATT_WIDTH, D_MODEL), 1),
            _const_spec((SSM_INNER, D_MODEL), 1),
            _const_spec((D_MODEL, D_MODEL), 1),
            _const_spec((1, D_MODEL), 1),
        ],
        out_specs=pl.BlockSpec((tm, D_MODEL), merge(0)),
        out_shape=jax.ShapeDtypeStruct((rows, D_MODEL), F32),
        scratch_shapes=[
            pltpu.VMEM((SSM_STATE, SSM_INNER), F32),
            pltpu.VMEM((tm // BLOCK, LANES, BLOCK), F32),
            pltpu.VMEM((tm, SSM_INNER), F32),
            pltpu.VMEM((2, tm, SSM_INNER), BF16),
            pltpu.VMEM((tm, ATT_WIDTH), BF16),
            pltpu.VMEM((tm, D_MODEL), F32),
            pltpu.VMEM((tm, D_MODEL), BF16),
        ],
        compiler_params=pltpu.CompilerParams(dimension_semantics=("arbitrary",),
                                             vmem_limit_bytes=VMEM_LIMIT_SSD_MERGE),
        name="ssd_merge",
    )(proj, dt, proj, dt_bias, a_log, d_skip, _expand_matrix(), s0, g_norm,
      o_att, proj, proj, x2, w_out_att, w_out_ssm, w_out, g_post)


def _transpose_cast_kernel(wt_ref, o_ref):
    o_ref[...] = wt_ref[...].T.astype(o_ref.dtype)


def _transpose_cast(w_t, n_features, *, block):
    assert n_features % block == 0 and block % LANES == 0
    return pl.pallas_call(
        _transpose_cast_kernel,
        grid=(n_features // block,),
        in_specs=[pl.BlockSpec((block, D_MODEL), lambda j: (j, 0))],
        out_specs=pl.BlockSpec((D_MODEL, block), lambda j: (0, j)),
        out_shape=jax.ShapeDtypeStruct((D_MODEL, n_features), BF16),
        compiler_params=pltpu.CompilerParams(vmem_limit_bytes=VMEM_LIMIT),
        name="transpose_cast_weight",
    )(w_t)


def _in_proj_weights(w_in):
    assert w_in.shape[1] == W_GATE_OFF + 2 * D_MODEL == sum(SPLIT_SIZES)
    w_t = w_in.T
    w_main = _transpose_cast(w_t, W_MAIN_WIDTH, block=W_MAIN_WIDTH // 5)
    n_tail = SSM_HEADS + 2 * D_MODEL
    w_gate, w_dt = pl.pallas_call(
        _tail_weights_kernel,
        grid=(1,),
        in_specs=[pl.BlockSpec((pl.Element(n_tail), pl.Element(D_MODEL)), lambda j: (W_DT_OFF, 0))],
        out_specs=[pl.BlockSpec((D_MODEL, 2 * D_MODEL), lambda j: (0, 0)),
                   pl.BlockSpec((D_MODEL, LANES), lambda j: (0, 0))],
        out_shape=[jax.ShapeDtypeStruct((D_MODEL, 2 * D_MODEL), BF16),
                   jax.ShapeDtypeStruct((D_MODEL, LANES), BF16)],
        compiler_params=pltpu.CompilerParams(vmem_limit_bytes=VMEM_LIMIT),
        name="transpose_cast_tail",
    )(w_t)
    return w_main, w_gate, w_dt


def _tail_weights_kernel(wt_ref, gate_ref, dt_ref):
    gate_ref[...] = wt_ref[SSM_HEADS:, :].T.astype(BF16)
    dt_cols = wt_ref[:LANES, :].T
    lane = lax.broadcasted_iota(jnp.int32, (1, LANES), 1)
    dt_ref[...] = jnp.where(lane < SSM_HEADS, dt_cols, 0.0).astype(BF16)


def _pad_lanes(v):
    return jnp.pad(v.astype(F32), (0, LANES - v.shape[0])).reshape(1, LANES)


def kernel(x, meta_tokens, g_pre, w_in, conv_w, conv_b, dt_bias, a_log, d_skip, attn_sinks, g_ssm_norm,
           w_out_att, w_out_ssm, w_out, g_post):
    b, seq, d = x.shape
    assert d == D_MODEL and PAD + N_META == BLOCK
    assert g_pre.shape[0] == 1, "one layer"
    in_tile, out_tile = 512, 512
    assert seq % in_tile == 0 and seq % out_tile == 0

    in_weights = _in_proj_weights(w_in[0])
    g_pre2 = g_pre[0].reshape(1, D_MODEL)
    conv_b2 = conv_b[0].reshape(1, CONV_DIM)
    ssd_params = (_pad_lanes(dt_bias[0]), _pad_lanes(a_log[0]),
                  jnp.repeat(d_skip[0].astype(F32), SSM_HEAD_DIM).reshape(1, SSM_INNER))

    h_meta = jnp.concatenate([jnp.zeros((PAD, D_MODEL), x.dtype), meta_tokens.astype(x.dtype)], axis=0)
    proj_m, dt_m, tail_m = _inproj(h_meta, g_pre2, *in_weights, conv_w[0], conv_b2,
                                   jnp.zeros((SUBLANES, CONV_DIM), F32),
                                   tm=BLOCK, tiles_per_seq=1, first_valid_row=PAD, emit_tail=True)
    state_m = _ssd_leading_state(proj_m, dt_m, *ssd_params)

    x2 = x.reshape(b * seq, D_MODEL)
    proj, dt, o_att = _inproj(x2, g_pre2, *in_weights, conv_w[0], conv_b2, tail_m,
                              (attn_sinks[0].astype(F32), *_meta_kv(proj_m)),
                              tm=in_tile, tiles_per_seq=seq // in_tile, first_valid_row=0, emit_tail=False)
    out = _ssd_merge(proj, dt, o_att, x2, *ssd_params, state_m, g_ssm_norm[0].reshape(1, SSM_INNER),
                     w_out_att[0].astype(BF16), w_out_ssm[0].astype(BF16), w_out[0].astype(BF16),
                     g_post[0].reshape(1, D_MODEL), tm=out_tile, tiles_per_seq=seq // out_tile)
    return out.reshape(b, seq, D_MODEL)
```

```python
import functools
import math

import numpy as np
import jax
import jax.numpy as jnp
from jax import lax
from jax.experimental import pallas as pl
from jax.experimental.pallas import tpu as pltpu

F32 = jnp.float32
BF16 = jnp.bfloat16

D_MODEL = 1024
N_META = 16
BLOCK = 128
PAD = (-N_META) % BLOCK
NORM_EPS = 1e-6

ATT_HEAD_DIM = 64
ATT_Q_HEADS = D_MODEL // ATT_HEAD_DIM
ATT_KV_HEADS = 4
ATT_GROUP = ATT_Q_HEADS // ATT_KV_HEADS
ATT_WIDTH = ATT_Q_HEADS * ATT_HEAD_DIM
KV_WIDTH = ATT_KV_HEADS * ATT_HEAD_DIM

SSM_INNER = 2 * D_MODEL
SSM_HEAD_DIM = 64
SSM_HEADS = SSM_INNER // SSM_HEAD_DIM
SSM_GROUPS = 4
SSM_HEADS_PER_GROUP = SSM_HEADS // SSM_GROUPS
SSM_GROUP_WIDTH = SSM_INNER // SSM_GROUPS
SSM_STATE = 128
CONV_WIDTH = 4
CONV_DIM = SSM_INNER + 2 * SSM_GROUPS * SSM_STATE

SPLIT_SIZES = (ATT_WIDTH, KV_WIDTH, KV_WIDTH, ATT_WIDTH, SSM_INNER, CONV_DIM, SSM_HEADS, D_MODEL, D_MODEL)

LANES = 128
SUBLANES = 8
LOG2E = math.log2(math.e)

PACK_WIDTH = ATT_WIDTH + ATT_WIDTH + SSM_INNER + 2 * D_MODEL + CONV_DIM + 2 * KV_WIDTH
Q_BLK = 0
ZATT_BLK = 1
ZSSM_BLK = 1
GATE_BLK = 2
XBC_BLK = 2
XBC_OFF = XBC_BLK * CONV_DIM
K_BLK = (PACK_WIDTH - 2 * KV_WIDTH) // KV_WIDTH
V_BLK = K_BLK + 1
PROJ_CHUNK = 256
CONV_ROWS = 64
OUT_CHUNK = 256
PIECE_ROWS = 64
W_Q_OFF = 0
W_KV_OFF = ATT_WIDTH
W_ZATT_OFF = W_KV_OFF + 2 * KV_WIDTH
W_ZSSM_OFF = W_ZATT_OFF + ATT_WIDTH
W_XBC_OFF = W_ZSSM_OFF + SSM_INNER
W_MAIN_WIDTH = W_XBC_OFF + CONV_DIM
W_DT_OFF = W_MAIN_WIDTH
W_GATE_OFF = W_DT_OFF + SSM_HEADS
VMEM_LIMIT = 48 * 1024 * 1024
VMEM_LIMIT_IN_PROJ = 58 * 1024 * 1024
VMEM_LIMIT_SSD_MERGE = 56 * 1024 * 1024


def _sigmoid(x):
    return 0.5 * jnp.tanh(0.5 * x) + 0.5


def _silu(x):
    half = 0.5 * x
    return half * jnp.tanh(half) + half


def _const_spec(shape, n_grid):
    zeros = (0,) * len(shape)
    index_map = (lambda i: zeros) if n_grid == 1 else (lambda i, j: zeros)
    return pl.BlockSpec(shape, index_map, pipeline_mode=pl.Buffered(1))


def _inproj_kernel(h_ref, g_ref, w_ref, wgate_ref, wdt_ref, convw_ref, convb_ref, tail0_ref, *rest,
                   tiles_per_seq, first_valid_row, emit_tail, with_attention):
    rest = list(rest)
    if with_attention:
        sink_ref, km_ref, vmt_ref = rest[:3]
        del rest[:3]
    proj_ref, dt_ref = rest[:2]
    del rest[:2]
    tailout_ref = rest.pop(0) if emit_tail else None
    o_ref = rest.pop(0) if with_attention else None
    tail_scr, raw_scr = rest[:2]
    if with_attention:
        kd_scr, vtd_scr = rest[2:]
    tm = h_ref.shape[0]
    i = pl.program_id(0)
    first_tile = i % tiles_per_seq == 0

    @pl.when(first_tile)
    def _():
        tail_scr[...] = tail0_ref[...]

    h = h_ref[...]
    ms = jnp.mean(h * h, axis=-1, keepdims=True)
    u = (h * lax.rsqrt(ms + NORM_EPS) * g_ref[...]).astype(BF16)
    dt_ref[...] = jnp.dot(u, wdt_ref[...], preferred_element_type=F32)

    def project(src_ref, c0):
        return jnp.dot(u, src_ref[:, c0:c0 + PROJ_CHUNK], preferred_element_type=F32)

    def plain_chunk(src_ref, src_c0, out_c0):
        r = project(src_ref, src_c0)
        if out_c0 < ZATT_BLK * ATT_WIDTH:
            r = r * (ATT_HEAD_DIM ** -0.5)
        elif out_c0 < GATE_BLK * 2 * D_MODEL:
            r = _silu(r)
        elif out_c0 < XBC_OFF:
            r = _sigmoid(r)
        proj_ref[:, out_c0:out_c0 + PROJ_CHUNK] = r.astype(BF16)

    def xbc_matmul(k):
        cols = slice(k * PROJ_CHUNK, (k + 1) * PROJ_CHUNK)
        raw = raw_scr.at[k % 2]
        raw[0:SUBLANES, :] = tail_scr[:, cols]
        raw[SUBLANES:, :] = project(w_ref, W_XBC_OFF + k * PROJ_CHUNK)
        new_tail = raw[tm:, :]
        tail_scr[:, cols] = new_tail
        if emit_tail:
            tailout_ref[:, cols] = new_tail

    def conv_piece(k, r0):
        raw = raw_scr.at[k % 2]
        for l0 in range(0, PROJ_CHUNK, LANES):
            cols = slice(k * PROJ_CHUNK + l0, k * PROJ_CHUNK + l0 + LANES)
            w0, w1, w2, w3 = (convw_ref[t:t + 1, cols] for t in range(CONV_WIDTH))
            ext = raw[r0:r0 + SUBLANES + CONV_ROWS, l0:l0 + LANES]
            ext1 = pltpu.roll(ext, 1, axis=0)
            pair_old = w1 * ext + w0 * ext1
            acc = (convb_ref[:, cols] + w3 * ext[SUBLANES:] + w2 * ext1[SUBLANES:]
                   + pltpu.roll(pair_old, 2, axis=0)[SUBLANES:])
            xact = _silu(acc)
            if first_valid_row:
                rows = lax.broadcasted_iota(jnp.int32, (CONV_ROWS, 1), 0) + r0
                xact = jnp.where(rows >= first_valid_row, xact, 0.0)
            proj_ref[r0:r0 + CONV_ROWS, XBC_OFF + cols.start:XBC_OFF + cols.stop] = xact.astype(BF16)

    if with_attention:
        n_blocks = tm // BLOCK
        n_lanes = ATT_GROUP * BLOCK
        key = lax.broadcasted_iota(jnp.int32, (BLOCK, n_lanes), 0)
        qry = lax.broadcasted_iota(jnp.int32, (BLOCK, n_lanes), 1) % BLOCK
        in_cur = key <= qry
        rel = jnp.where(in_cur, qry - key, qry - key + BLOCK).astype(F32)
        rel_first = jnp.where(jnp.logical_or(in_cur, jnp.logical_not(first_tile)), rel, jnp.inf)
        cur_b = in_cur.astype(BF16)
        prev_b = 1.0 - cur_b
        low_lanes = lax.broadcasted_iota(jnp.int32, (1, LANES), 1) < ATT_HEAD_DIM
        low_rows = lax.broadcasted_iota(jnp.int32, (LANES, 1), 0) < ATT_HEAD_DIM
        nt = (((1,), (1,)), ((), ()))

        @pl.when(first_tile)
        def _():
            kd_scr[0] = jnp.zeros(kd_scr.shape[1:], BF16)
            vtd_scr[0] = jnp.zeros(vtd_scr.shape[1:], BF16)

        @pl.when(jnp.logical_not(first_tile))
        def _():
            kd_scr[0] = kd_scr[n_blocks]
            vtd_scr[0] = vtd_scr[n_blocks]

    def arrange_kv(blk):
        rows = slice(blk * BLOCK, (blk + 1) * BLOCK)
        k_f = proj_ref[rows, K_BLK * KV_WIDTH:(K_BLK + 1) * KV_WIDTH].astype(F32)
        v_f = proj_ref[rows, V_BLK * KV_WIDTH:(V_BLK + 1) * KV_WIDTH].astype(F32)
        for t in range(KV_WIDTH // LANES):
            k_tile = k_f[:, t * LANES:(t + 1) * LANES]
            k_swap = pltpu.roll(k_tile, ATT_HEAD_DIM, axis=1)
            kd_scr[blk + 1, 2 * t] = jnp.where(low_lanes, k_tile, k_swap).astype(BF16)
            kd_scr[blk + 1, 2 * t + 1] = jnp.where(low_lanes, k_swap, k_tile).astype(BF16)
            vt = v_f[:, t * LANES:(t + 1) * LANES].T
            vtd_scr[blk + 1, 2 * t] = jnp.concatenate([vt[:ATT_HEAD_DIM]] * 2, axis=0).astype(BF16)
            vtd_scr[blk + 1, 2 * t + 1] = jnp.concatenate([vt[ATT_HEAD_DIM:]] * 2, axis=0).astype(BF16)

    def attend(blk, g):
        rows = slice(blk * BLOCK, (blk + 1) * BLOCK)
        k_dup = jnp.concatenate([kd_scr[blk, g], kd_scr[blk + 1, g]], axis=0)
        vt_dup = jnp.concatenate([vtd_scr[blk, g], vtd_scr[blk + 1, g]], axis=1)
        q_rows = []
        for t in (2 * g, 2 * g + 1):
            q_tile = proj_ref[rows, Q_BLK * ATT_WIDTH + t * LANES:Q_BLK * ATT_WIDTH + (t + 1) * LANES]
            q_rows += [jnp.where(low_lanes, q_tile, 0), jnp.where(low_lanes, 0, q_tile)]
        q_cat = jnp.concatenate(q_rows, axis=0)
        s_band = lax.dot_general(k_dup, q_cat, nt, preferred_element_type=F32)
        s_meta = lax.dot_general(km_ref[g], q_cat, nt, preferred_element_type=F32)

        heads = range(g * ATT_GROUP, (g + 1) * ATT_GROUP)
        slopes = jnp.concatenate(
            [jnp.full((1, BLOCK), 2.0 ** (-8.0 * (h + 1) / ATT_Q_HEADS), F32) for h in heads], axis=1)
        sinks = jnp.concatenate([jnp.full((1, BLOCK), sink_ref[h], F32) for h in heads], axis=1)
        s = jnp.where(in_cur, s_band[BLOCK:], s_band[:BLOCK]) - slopes * (rel_first if blk == 0 else rel)
        m = jnp.maximum(jnp.max(s, axis=0, keepdims=True), jnp.max(s_meta, axis=0, keepdims=True))
        m = jnp.maximum(m, sinks)
        p = jnp.exp(s - m)
        p_meta = jnp.exp(s_meta - m)
        denom = (jnp.sum(p, axis=0, keepdims=True) + jnp.sum(p_meta, axis=0, keepdims=True)
                 + jnp.exp(sinks - m))
        p_b = p.astype(BF16)
        p_band = jnp.concatenate([p_b * prev_b, p_b * cur_b], axis=0)
        o_t = jnp.dot(vt_dup, p_band, preferred_element_type=F32)
        o_t = o_t + jnp.dot(vmt_ref[g], p_meta.astype(BF16), preferred_element_type=F32)
        o_t = o_t * (1.0 / denom)
        for pair in range(ATT_GROUP // 2):
            head_a = o_t[:, (2 * pair) * BLOCK:(2 * pair + 1) * BLOCK]
            head_b = o_t[:, (2 * pair + 1) * BLOCK:(2 * pair + 2) * BLOCK]
            tile = 2 * g + pair
            o_ref[rows, tile * LANES:(tile + 1) * LANES] = jnp.where(low_rows, head_a, head_b).T.astype(BF16)

    early, late = _plain_chunks(w_ref, wgate_ref)
    for chunk in early:
        plain_chunk(*chunk)
    attn_pending = []
    if with_attention:
        for blk in range(n_blocks):
            arrange_kv(blk)
        attn_pending = [(blk, g) for blk in range(n_blocks) for g in range(ATT_KV_HEADS)]
    n_xbc = CONV_DIM // PROJ_CHUNK
    n_items = n_xbc + len(late)
    xbc_at = {k * n_items // n_xbc: k for k in range(n_xbc)}
    late_iter = iter(late)
    matmuls = [xbc_at[n] if n in xbc_at else next(late_iter) for n in range(n_items)]
    conv_per_matmul = -(-n_xbc * (tm // CONV_ROWS) // (len(matmuls) - 1))
    attn_per_matmul = -(-len(attn_pending) // len(matmuls))
    conv_pending = []
    for item in matmuls:
        if isinstance(item, int):
            assert all(k > item - 2 for k, _ in conv_pending), "raw_scr slot still in use"
            xbc_matmul(item)
            conv_pending += [(item, r0) for r0 in range(0, tm, CONV_ROWS)]
        else:
            plain_chunk(*item)
        for piece in conv_pending[:conv_per_matmul]:
            conv_piece(*piece)
        conv_pending = conv_pending[conv_per_matmul:]
        for piece in attn_pending[:attn_per_matmul]:
            attend(*piece)
        attn_pending = attn_pending[attn_per_matmul:]
    for piece in conv_pending:
        conv_piece(*piece)
    for piece in attn_pending:
        attend(*piece)


def _plain_chunks(w_ref, wgate_ref):
    def chunks(ref, src, dst, width):
        return [(ref, src + c, dst + c) for c in range(0, width, PROJ_CHUNK)]

    early = chunks(w_ref, W_Q_OFF, Q_BLK * ATT_WIDTH, ATT_WIDTH) + chunks(w_ref, W_KV_OFF, K_BLK * KV_WIDTH, 2 * KV_WIDTH)
    late = (chunks(w_ref, W_ZATT_OFF, ZATT_BLK * ATT_WIDTH, ATT_WIDTH)
            + chunks(w_ref, W_ZSSM_OFF, ZSSM_BLK * SSM_INNER, SSM_INNER)
            + chunks(wgate_ref, 0, GATE_BLK * 2 * D_MODEL, 2 * D_MODEL))
    return early, late


def _inproj(h2, g_pre, w_main, w_gate, w_dt, conv_w, conv_b, tail0, attention=None, *, tm, tiles_per_seq,
            first_valid_row, emit_tail):
    rows = h2.shape[0]
    assert tm % CONV_ROWS == 0 and tm % BLOCK == 0
    operands = [h2, g_pre, w_main, w_gate, w_dt, conv_w, conv_b, tail0]
    in_specs = [
        pl.BlockSpec((tm, D_MODEL), lambda i: (i, 0)),
        _const_spec((1, D_MODEL), 1),
        _const_spec((D_MODEL, W_MAIN_WIDTH), 1),
        _const_spec((D_MODEL, 2 * D_MODEL), 1),
        _const_spec((D_MODEL, LANES), 1),
        _const_spec((CONV_WIDTH, CONV_DIM), 1),
        _const_spec((1, CONV_DIM), 1),
        _const_spec((SUBLANES, CONV_DIM), 1),
    ]
    out_specs = [pl.BlockSpec((tm, PACK_WIDTH), lambda i: (i, 0)), pl.BlockSpec((tm, LANES), lambda i: (i, 0))]
    out_shape = [jax.ShapeDtypeStruct((rows, PACK_WIDTH), BF16), jax.ShapeDtypeStruct((rows, LANES), F32)]
    scratch = [pltpu.VMEM((SUBLANES, CONV_DIM), F32),
               pltpu.VMEM((2, SUBLANES + tm, PROJ_CHUNK), F32)]
    if emit_tail:
        assert rows == tm
        out_specs.append(pl.BlockSpec((SUBLANES, CONV_DIM), lambda i: (0, 0)))
        out_shape.append(jax.ShapeDtypeStruct((SUBLANES, CONV_DIM), F32))
    if attention is not None:
        operands += list(attention)
        in_specs += [pl.BlockSpec(memory_space=pltpu.SMEM),
                     _const_spec((ATT_KV_HEADS, N_META, LANES), 1),
                     _const_spec((ATT_KV_HEADS, LANES, N_META), 1)]
        out_specs.append(pl.BlockSpec((tm, ATT_WIDTH), lambda i: (i, 0)))
        out_shape.append(jax.ShapeDtypeStruct((rows, ATT_WIDTH), BF16))
        kv_slots = (tm // BLOCK + 1, ATT_KV_HEADS, BLOCK, LANES)
        scratch += [pltpu.VMEM(kv_slots, BF16), pltpu.VMEM(kv_slots, BF16)]
    return pl.pallas_call(
        functools.partial(_inproj_kernel, tiles_per_seq=tiles_per_seq, first_valid_row=first_valid_row,
                          emit_tail=emit_tail, with_attention=attention is not None),
        grid=(rows // tm,),
        in_specs=in_specs,
        out_specs=out_specs,
        out_shape=out_shape,
        scratch_shapes=scratch,
        compiler_params=pltpu.CompilerParams(dimension_semantics=("arbitrary",),
                                             vmem_limit_bytes=VMEM_LIMIT_IN_PROJ),
        name="in_proj",
    )(*operands)


def _meta_kv(proj_meta):
    k_off, v_off = K_BLK * KV_WIDTH, V_BLK * KV_WIDTH
    k_dup, vt_dup = [], []
    for g in range(ATT_KV_HEADS):
        kg = proj_meta[PAD:, k_off + g * ATT_HEAD_DIM:k_off + (g + 1) * ATT_HEAD_DIM]
        vg = proj_meta[PAD:, v_off + g * ATT_HEAD_DIM:v_off + (g + 1) * ATT_HEAD_DIM]
        k_dup.append(jnp.concatenate([kg, kg], axis=1))
        vt_dup.append(jnp.concatenate([vg.T, vg.T], axis=0))
    return jnp.stack(k_dup), jnp.stack(vt_dup)


def _split3(a):
    hi = a.astype(BF16).astype(F32)
    r1 = a - hi
    mid = r1.astype(BF16).astype(F32)
    return hi, mid, r1 - mid


def _expand_lhs(x):
    hi, mid, lo = _split3(x)
    return (hi + pltpu.roll(mid, SSM_HEADS, axis=1) + pltpu.roll(lo, 2 * SSM_HEADS, axis=1)).astype(BF16)


def _expand_matrix():
    r = np.arange(LANES)[:, None]
    c = np.arange(SSM_INNER)[None, :]
    return jnp.asarray((r < 3 * SSM_HEADS) & (r % SSM_HEADS == c // SSM_HEAD_DIM), BF16)


def _ssd_chunk_fns(xact_ref, dt_ref, dtb_ref, alog_ref, dskip_ref, expand_ref, state_scr, q2row_scr, y_ref):
    emit_y = y_ref is not None
    row = lax.broadcasted_iota(jnp.int32, (BLOCK, BLOCK), 0)
    col = lax.broadcasted_iota(jnp.int32, (BLOCK, BLOCK), 1)
    tril = row >= col
    tril_b = tril.astype(BF16)
    lane = lax.broadcasted_iota(jnp.int32, (1, LANES), 1)
    head_lane = lane < SSM_HEADS
    low_lane = lane < SSM_HEAD_DIM
    low_b = low_lane.astype(BF16)
    high_b = 1.0 - low_b
    neg_a2 = -jnp.exp(alog_ref[...]) * LOG2E
    nt = (((1,), (1,)), ((), ()))
    n_bc = SSM_GROUPS * SSM_STATE

    def decays(ch):
        rows = slice(ch * BLOCK, (ch + 1) * BLOCK)
        x_dt = dt_ref[rows, :] + dtb_ref[...]
        dt = jnp.maximum(x_dt, 0.0) + jnp.log1p(jnp.exp(-jnp.abs(x_dt)))
        a2 = dt * neg_a2
        cs3 = jnp.dot(tril_b, jnp.concatenate(_split3(a2), axis=1).astype(BF16), preferred_element_type=F32)
        cs2 = cs3[:, :LANES] + cs3[:, LANES:2 * LANES] + cs3[:, 2 * LANES:]
        q2 = cs2 - jnp.log(dt) * LOG2E
        cs2_last = cs2[BLOCK - 1:BLOCK, :]
        q2row_scr[ch] = q2.T
        w = jnp.where(head_lane, jnp.exp2(cs2_last - q2), 0.0)
        cd = jnp.where(head_lane, jnp.exp2(cs2_last), 0.0)
        parts = [_expand_lhs(w), _expand_lhs(jnp.broadcast_to(cd, (2 * SUBLANES, LANES)))]
        ex = jnp.dot(jnp.concatenate(parts, axis=0), expand_ref[...], preferred_element_type=F32)
        return cs2, ex[:BLOCK], ex[BLOCK:BLOCK + 1]

    def group(ch, g, cs2, w_exp, cd_exp):
        rows = slice(ch * BLOCK, (ch + 1) * BLOCK)
        gcols = slice(g * SSM_GROUP_WIDTH, (g + 1) * SSM_GROUP_WIDTH)
        bg = xact_ref[rows,SSM_INNER + g * SSM_STATE:SSM_INNER + (g + 1) * SSM_STATE]
        cg = xact_ref[rows,SSM_INNER + n_bc + g * SSM_STATE:SSM_INNER + n_bc + (g + 1) * SSM_STATE]
        xs_f = xact_ref[rows,gcols].astype(F32)
        s_prev = state_scr[:, gcols]
        bg_t = bg.astype(F32).T.astype(BF16)
        new_states = jnp.dot(bg_t, (xs_f * w_exp[:, gcols]).astype(BF16), preferred_element_type=F32)
        state_scr[:, gcols] = s_prev * cd_exp[:, gcols] + new_states
        if not emit_y:
            return
        cb = lax.dot_general(cg, bg, nt, preferred_element_type=F32)
        y_off = jnp.dot(cg, s_prev.astype(BF16), preferred_element_type=F32)
        for pair in range(SSM_HEADS_PER_GROUP // 2):
            h0 = g * SSM_HEADS_PER_GROUP + 2 * pair
            cols = slice(h0 * SSM_HEAD_DIM, (h0 + 2) * SSM_HEAD_DIM)
            pcols = slice(2 * pair * SSM_HEAD_DIM, (2 * pair + 2) * SSM_HEAD_DIM)
            m_parts, from_start = [], []
            for h in (h0, h0 + 1):
                cs_l = jnp.broadcast_to(cs2[:, h:h + 1], (BLOCK, BLOCK))
                expo = jnp.where(tril, cs_l - q2row_scr[ch, h:h + 1, :], -jnp.inf)
                m_parts.append((cb * jnp.exp2(expo)).astype(BF16))
                from_start.append(jnp.exp2(cs_l))
            e_pair = jnp.where(low_lane, from_start[0], from_start[1])
            x_pair = xact_ref[rows,cols]
            x_bd = jnp.concatenate([x_pair * low_b, x_pair * high_b], axis=0)
            y_pair = jnp.dot(jnp.concatenate(m_parts, axis=1), x_bd, preferred_element_type=F32)
            y_pair = y_pair + e_pair * y_off[:, pcols] + dskip_ref[:, cols] * xs_f[:, pcols]
            y_ref[rows, cols] = y_pair.astype(y_ref.dtype)

    return decays, group


def _ssd_state_kernel(xact_ref, dt_ref, dtb_ref, alog_ref, dskip_ref, expand_ref, sout_ref, q2row_scr):
    sout_ref[...] = jnp.zeros(sout_ref.shape, F32)
    decays, group = _ssd_chunk_fns(xact_ref, dt_ref, dtb_ref, alog_ref, dskip_ref, expand_ref, sout_ref,
                                   q2row_scr, None)
    terms = decays(0)
    for g in range(SSM_GROUPS):
        group(0, g, *terms)


def _ssd_leading_state(proj_m, dt_m, dt_bias, a_log, d_skip):
    return pl.pallas_call(
        _ssd_state_kernel,
        grid=(1,),
        in_specs=[
            pl.BlockSpec((BLOCK, CONV_DIM), lambda i: (0, XBC_BLK)),
            _const_spec((BLOCK, LANES), 1),
            _const_spec((1, LANES), 1),
            _const_spec((1, LANES), 1),
            _const_spec((1, SSM_INNER), 1),
            _const_spec((LANES, SSM_INNER), 1),
        ],
        out_specs=pl.BlockSpec((SSM_STATE, SSM_INNER), lambda i: (0, 0)),
        out_shape=jax.ShapeDtypeStruct((SSM_STATE, SSM_INNER), F32),
        scratch_shapes=[pltpu.VMEM((1, LANES, BLOCK), F32)],
        compiler_params=pltpu.CompilerParams(vmem_limit_bytes=VMEM_LIMIT),
        name="ssd_leading_state",
    )(proj_m, dt_m, dt_bias, a_log, d_skip, _expand_matrix())


def _ssd_merge_kernel(xact_ref, dt_ref, zssm_ref, dtb_ref, alog_ref, dskip_ref, expand_ref, s0_ref, gnorm_ref,
                      o_ref, zatt_ref, gate_ref, x_ref, woa_ref, wos_ref, wo_ref, gpost_ref,
                      out_ref, state_scr, q2row_scr, y_scr, ssm_scr, att_scr, ya_scr, merged_scr,
                      *, tiles_per_seq, n_tiles):
    tm = x_ref.shape[0]
    i = pl.program_id(0)
    scan_tile = jnp.minimum(i, n_tiles - 1)
    scan_slot = i % 2
    merge_slot = (i + 1) % 2

    @pl.when(scan_tile % tiles_per_seq == 0)
    def _():
        state_scr[...] = s0_ref[...]

    @pl.when(i == 0)
    def _():
        ssm_scr[1] = jnp.zeros(ssm_scr.shape[1:], BF16)

    decays, group = _ssd_chunk_fns(xact_ref, dt_ref, dtb_ref, alog_ref, dskip_ref, expand_ref, state_scr,
                                   q2row_scr, y_scr)

    def branch_piece(ch):
        rows = slice(ch * BLOCK, (ch + 1) * BLOCK)
        y = y_scr[rows, :] * zssm_ref[rows, :].astype(F32)
        for g in range(SSM_GROUPS):
            cols = slice(g * SSM_GROUP_WIDTH, (g + 1) * SSM_GROUP_WIDTH)
            yg = y[:, cols]
            yg = yg * lax.rsqrt(jnp.mean(yg * yg, axis=-1, keepdims=True) + NORM_EPS)
            ssm_scr[scan_slot, rows, cols] = (yg * gnorm_ref[:, cols]).astype(BF16)

    def att_piece(r0):
        rows = slice(r0, r0 + PIECE_ROWS)
        att_scr[rows, :] = (o_ref[rows, :].astype(F32) * zatt_ref[rows, :].astype(F32)).astype(BF16)

    def att_chunk(c0):
        cols = slice(c0, c0 + OUT_CHUNK)
        ya_scr[:, cols] = jnp.dot(att_scr[...], woa_ref[:, cols], preferred_element_type=F32)

    def ssm_merge_chunk(c0):
        cols = slice(c0, c0 + OUT_CHUNK)
        ys = jnp.dot(ssm_scr[merge_slot], wos_ref[:, cols], preferred_element_type=F32)
        gate_att = gate_ref[:, cols].astype(F32)
        gate_ssm = gate_ref[:, D_MODEL + c0:D_MODEL + c0 + OUT_CHUNK].astype(F32)
        merged_scr[:, cols] = (gate_att * ya_scr[:, cols] + gate_ssm * ys).astype(BF16)

    def out_chunk(c0):
        cols = slice(c0, c0 + OUT_CHUNK)
        out_ref[:, cols] = jnp.dot(merged_scr[...], wo_ref[:, cols], preferred_element_type=F32)

    def finish():
        out = out_ref[...]
        ms = jnp.mean(out * out, axis=-1, keepdims=True)
        out_ref[...] = x_ref[...] + out * lax.rsqrt(ms + NORM_EPS) * gpost_ref[...]

    chunks = range(0, D_MODEL, OUT_CHUNK)
    pieces = [functools.partial(att_piece, r0) for r0 in range(0, tm, PIECE_ROWS)]
    matmuls = ([functools.partial(att_chunk, c0) for c0 in chunks]
               + [functools.partial(ssm_merge_chunk, c0) for c0 in chunks]
               + [functools.partial(out_chunk, c0) for c0 in chunks] + [finish])

    n_chunks = tm // BLOCK
    n_slots = n_chunks * SSM_GROUPS
    lead = 2
    per_lead = -(-len(pieces) // lead)
    per_slot = -(-len(matmuls) // (n_slots - lead))
    terms = decays(0)
    slot = 0
    for ch in range(n_chunks):
        next_terms = None
        for g in range(SSM_GROUPS):
            group(ch, g, *terms)
            if g == SSM_GROUPS // 2 - 1 and ch + 1 < n_chunks:
                next_terms = decays(ch + 1)
            if slot < lead:
                todo, pieces = pieces[:per_lead], pieces[per_lead:]
            else:
                todo, matmuls = matmuls[:per_slot], matmuls[per_slot:]
            for piece in todo:
                piece()
            slot += 1
        branch_piece(ch)
        terms = next_terms
    for piece in pieces + matmuls:
        piece()


def _ssd_merge(proj, dt, o_att, x2, dt_bias, a_log, d_skip, s0, g_norm, w_out_att, w_out_ssm, w_out, g_post,
               *, tm, tiles_per_seq):
    rows = x2.shape[0]
    n_tiles = rows // tm
    scan = lambda col: (lambda i: (jnp.minimum(i, n_tiles - 1), col))
    merge = lambda col: (lambda i: (jnp.maximum(i - 1, 0), col))
    return pl.pallas_call(
        functools.partial(_ssd_merge_kernel, tiles_per_seq=tiles_per_seq, n_tiles=n_tiles),
        grid=(n_tiles + 1,),
        in_specs=[
            pl.BlockSpec((tm, CONV_DIM), scan(XBC_BLK)),
            pl.BlockSpec((tm, LANES), scan(0)),
            pl.BlockSpec((tm, SSM_INNER), scan(ZSSM_BLK)),
            _const_spec((1, LANES), 1),
            _const_spec((1, LANES), 1),
            _const_spec((1, SSM_INNER), 1),
            _const_spec((LANES, SSM_INNER), 1),
            _const_spec((SSM_STATE, SSM_INNER), 1),
            _const_spec((1, SSM_INNER), 1),
            pl.BlockSpec((tm, ATT_WIDTH), merge(0)),
            pl.BlockSpec((tm, ATT_WIDTH), merge(ZATT_BLK)),
            pl.BlockSpec((tm, 2 * D_MODEL), merge(GATE_BLK)),
            pl.BlockSpec((tm, D_MODEL), merge(0)),
            _const_spec((ATT_WIDTH, D_MODEL), 1),
            _const_spec((SSM_INNER, D_MODEL), 1),
            _const_spec((D_MODEL, D_MODEL), 1),
            _const_spec((1, D_MODEL), 1),
        ],
        out_specs=pl.BlockSpec((tm, D_MODEL), merge(0)),
        out_shape=jax.ShapeDtypeStruct((rows, D_MODEL), F32),
        scratch_shapes=[
            pltpu.VMEM((SSM_STATE, SSM_INNER), F32),
            pltpu.VMEM((tm // BLOCK, LANES, BLOCK), F32),
            pltpu.VMEM((tm, SSM_INNER), F32),
            pltpu.VMEM((2, tm, SSM_INNER), BF16),
            pltpu.VMEM((tm, ATT_WIDTH), BF16),
            pltpu.VMEM((tm, D_MODEL), F32),
            pltpu.VMEM((tm, D_MODEL), BF16),
        ],
        compiler_params=pltpu.CompilerParams(dimension_semantics=("arbitrary",),
                                             vmem_limit_bytes=VMEM_LIMIT_SSD_MERGE),
        name="ssd_merge",
    )(proj, dt, proj, dt_bias, a_log, d_skip, _expand_matrix(), s0, g_norm,
      o_att, proj, proj, x2, w_out_att, w_out_ssm, w_out, g_post)


def _transpose_cast_kernel(wt_ref, o_ref):
    o_ref[...] = wt_ref[...].T.astype(o_ref.dtype)


def _transpose_cast(w_t, n_features, *, block):
    assert n_features % block == 0 and block % LANES == 0
    return pl.pallas_call(
        _transpose_cast_kernel,
        grid=(n_features // block,),
        in_specs=[pl.BlockSpec((block, D_MODEL), lambda j: (j, 0))],
        out_specs=pl.BlockSpec((D_MODEL, block), lambda j: (0, j)),
        out_shape=jax.ShapeDtypeStruct((D_MODEL, n_features), BF16),
        compiler_params=pltpu.CompilerParams(vmem_limit_bytes=VMEM_LIMIT),
        name="transpose_cast_weight",
    )(w_t)


def _in_proj_weights(w_in):
    assert w_in.shape[1] == W_GATE_OFF + 2 * D_MODEL == sum(SPLIT_SIZES)
    w_t = w_in.T
    w_main = _transpose_cast(w_t, W_MAIN_WIDTH, block=W_MAIN_WIDTH // 5)
    n_tail = SSM_HEADS + 2 * D_MODEL
    w_gate, w_dt = pl.pallas_call(
        _tail_weights_kernel,
        grid=(1,),
        in_specs=[pl.BlockSpec((pl.Element(n_tail), pl.Element(D_MODEL)), lambda j: (W_DT_OFF, 0))],
        out_specs=[pl.BlockSpec((D_MODEL, 2 * D_MODEL), lambda j: (0, 0)),
                   pl.BlockSpec((D_MODEL, LANES), lambda j: (0, 0))],
        out_shape=[jax.ShapeDtypeStruct((D_MODEL, 2 * D_MODEL), BF16),
                   jax.ShapeDtypeStruct((D_MODEL, LANES), BF16)],
        compiler_params=pltpu.CompilerParams(vmem_limit_bytes=VMEM_LIMIT),
        name="transpose_cast_tail",
    )(w_t)
    return w_main, w_gate, w_dt


def _tail_weights_kernel(wt_ref, gate_ref, dt_ref):
    gate_ref[...] = wt_ref[SSM_HEADS:, :].T.astype(BF16)
    dt_cols = wt_ref[:LANES, :].T
    lane = lax.broadcasted_iota(jnp.int32, (1, LANES), 1)
    dt_ref[...] = jnp.where(lane < SSM_HEADS, dt_cols, 0.0).astype(BF16)


def _pad_lanes(v):
    return jnp.pad(v.astype(F32), (0, LANES - v.shape[0])).reshape(1, LANES)


def kernel(x, meta_tokens, g_pre, w_in, conv_w, conv_b, dt_bias, a_log, d_skip, attn_sinks, g_ssm_norm,
           w_out_att, w_out_ssm, w_out, g_post):
    b, seq, d = x.shape
    assert d == D_MODEL and PAD + N_META == BLOCK
    assert g_pre.shape[0] == 1, "one layer"
    in_tile, out_tile = 512, 512
    assert seq % in_tile == 0 and seq % out_tile == 0

    in_weights = _in_proj_weights(w_in[0])
    g_pre2 = g_pre[0].reshape(1, D_MODEL)
    conv_b2 = conv_b[0].reshape(1, CONV_DIM)
    ssd_params = (_pad_lanes(dt_bias[0]), _pad_lanes(a_log[0]),
                  jnp.repeat(d_skip[0].astype(F32), SSM_HEAD_DIM).reshape(1, SSM_INNER))

    h_meta = jnp.concatenate([jnp.zeros((PAD, D_MODEL), x.dtype), meta_tokens.astype(x.dtype)], axis=0)
    proj_m, dt_m, tail_m = _inproj(h_meta, g_pre2, *in_weights, conv_w[0], conv_b2,
                                   jnp.zeros((SUBLANES, CONV_DIM), F32),
                                   tm=BLOCK, tiles_per_seq=1, first_valid_row=PAD, emit_tail=True)
    state_m = _ssd_leading_state(proj_m, dt_m, *ssd_params)

    x2 = x.reshape(b * seq, D_MODEL)
    proj, dt, o_att = _inproj(x2, g_pre2, *in_weights, conv_w[0], conv_b2, tail_m,
                              (attn_sinks[0].astype(F32), *_meta_kv(proj_m)),
                              tm=in_tile, tiles_per_seq=seq // in_tile, first_valid_row=0, emit_tail=False)
    out = _ssd_merge(proj, dt, o_att, x2, *ssd_params, state_m, g_ssm_norm[0].reshape(1, SSM_INNER),
                     w_out_att[0].astype(BF16), w_out_ssm[0].astype(BF16), w_out[0].astype(BF16),
                     g_post[0].reshape(1, D_MODEL), tm=out_tile, tiles_per_seq=seq // out_tile)
    return out.reshape(b, seq, D_MODEL)
```

```python
import functools
import math

import numpy as np
import jax
import jax.numpy as jnp
from jax import lax
from jax.experimental import pallas as pl
from jax.experimental.pallas import tpu as pltpu

F32 = jnp.float32
BF16 = jnp.bfloat16

D_MODEL = 1024
N_META = 16
BLOCK = 128
PAD = (-N_META) % BLOCK
NORM_EPS = 1e-6

ATT_HEAD_DIM = 64
ATT_Q_HEADS = D_MODEL // ATT_HEAD_DIM
ATT_KV_HEADS = 4
ATT_GROUP = ATT_Q_HEADS // ATT_KV_HEADS
ATT_WIDTH = ATT_Q_HEADS * ATT_HEAD_DIM
KV_WIDTH = ATT_KV_HEADS * ATT_HEAD_DIM

SSM_INNER = 2 * D_MODEL
SSM_HEAD_DIM = 64
SSM_HEADS = SSM_INNER // SSM_HEAD_DIM
SSM_GROUPS = 4
SSM_HEADS_PER_GROUP = SSM_HEADS // SSM_GROUPS
SSM_GROUP_WIDTH = SSM_INNER // SSM_GROUPS
SSM_STATE = 128
CONV_WIDTH = 4
CONV_DIM = SSM_INNER + 2 * SSM_GROUPS * SSM_STATE

SPLIT_SIZES = (ATT_WIDTH, KV_WIDTH, KV_WIDTH, ATT_WIDTH, SSM_INNER, CONV_DIM, SSM_HEADS, D_MODEL, D_MODEL)

LANES = 128
SUBLANES = 8
LOG2E = math.log2(math.e)

PACK_WIDTH = ATT_WIDTH + ATT_WIDTH + SSM_INNER + 2 * D_MODEL + CONV_DIM + 2 * KV_WIDTH
Q_BLK = 0
ZATT_BLK = 1
ZSSM_BLK = 1
GATE_BLK = 2
XBC_BLK = 2
XBC_OFF = XBC_BLK * CONV_DIM
K_BLK = (PACK_WIDTH - 2 * KV_WIDTH) // KV_WIDTH
V_BLK = K_BLK + 1
PROJ_CHUNK = 256
CONV_ROWS = 64
OUT_CHUNK = 256
PIECE_ROWS = 64
W_Q_OFF = 0
W_KV_OFF = ATT_WIDTH
W_ZATT_OFF = W_KV_OFF + 2 * KV_WIDTH
W_ZSSM_OFF = W_ZATT_OFF + ATT_WIDTH
W_XBC_OFF = W_ZSSM_OFF + SSM_INNER
W_MAIN_WIDTH = W_XBC_OFF + CONV_DIM
W_DT_OFF = W_MAIN_WIDTH
W_GATE_OFF = W_DT_OFF + SSM_HEADS
VMEM_LIMIT = 48 * 1024 * 1024
VMEM_LIMIT_IN_PROJ = 58 * 1024 * 1024
VMEM_LIMIT_SSD_MERGE = 56 * 1024 * 1024


def _sigmoid(x):
    return 0.5 * jnp.tanh(0.5 * x) + 0.5


def _silu(x):
    half = 0.5 * x
    return half * jnp.tanh(half) + half


def _const_spec(shape, n_grid):
    zeros = (0,) * len(shape)
    index_map = (lambda i: zeros) if n_grid == 1 else (lambda i, j: zeros)
    return pl.BlockSpec(shape, index_map, pipeline_mode=pl.Buffered(1))


def _inproj_kernel(h_ref, g_ref, w_ref, wgate_ref, wdt_ref, convw_ref, convb_ref, tail0_ref, *rest,
                   tiles_per_seq, first_valid_row, emit_tail, with_attention):
    rest = list(rest)
    if with_attention:
        sink_ref, km_ref, vmt_ref = rest[:3]
        del rest[:3]
    proj_ref, dt_ref = rest[:2]
    del rest[:2]
    tailout_ref = rest.pop(0) if emit_tail else None
    o_ref = rest.pop(0) if with_attention else None
    tail_scr, raw_scr = rest[:2]
    if with_attention:
        kd_scr, vtd_scr = rest[2:]
    tm = h_ref.shape[0]
    i = pl.program_id(0)
    first_tile = i % tiles_per_seq == 0

    @pl.when(first_tile)
    def _():
        tail_scr[...] = tail0_ref[...]

    h = h_ref[...]
    ms = jnp.mean(h * h, axis=-1, keepdims=True)
    u = (h * lax.rsqrt(ms + NORM_EPS) * g_ref[...]).astype(BF16)
    dt_ref[...] = jnp.dot(u, wdt_ref[...], preferred_element_type=F32)

    def project(src_ref, c0):
        return jnp.dot(u, src_ref[:, c0:c0 + PROJ_CHUNK], preferred_element_type=F32)

    def plain_chunk(src_ref, src_c0, out_c0):
        r = project(src_ref, src_c0)
        if out_c0 < ZATT_BLK * ATT_WIDTH:
            r = r * (ATT_HEAD_DIM ** -0.5)
        elif out_c0 < GATE_BLK * 2 * D_MODEL:
            r = _silu(r)
        elif out_c0 < XBC_OFF:
            r = _sigmoid(r)
        proj_ref[:, out_c0:out_c0 + PROJ_CHUNK] = r.astype(BF16)

    def xbc_matmul(k):
        cols = slice(k * PROJ_CHUNK, (k + 1) * PROJ_CHUNK)
        raw = raw_scr.at[k % 2]
        raw[0:SUBLANES, :] = tail_scr[:, cols]
        raw[SUBLANES:, :] = project(w_ref, W_XBC_OFF + k * PROJ_CHUNK)
        new_tail = raw[tm:, :]
        tail_scr[:, cols] = new_tail
        if emit_tail:
            tailout_ref[:, cols] = new_tail

    def conv_piece(k, r0):
        raw = raw_scr.at[k % 2]
        for l0 in range(0, PROJ_CHUNK, LANES):
            cols = slice(k * PROJ_CHUNK + l0, k * PROJ_CHUNK + l0 + LANES)
            w0, w1, w2, w3 = (convw_ref[t:t + 1, cols] for t in range(CONV_WIDTH))
            ext = raw[r0:r0 + SUBLANES + CONV_ROWS, l0:l0 + LANES]
            ext1 = pltpu.roll(ext, 1, axis=0)
            pair_old = w1 * ext + w0 * ext1
            acc = (convb_ref[:, cols] + w3 * ext[SUBLANES:] + w2 * ext1[SUBLANES:]
                   + pltpu.roll(pair_old, 2, axis=0)[SUBLANES:])
            xact = _silu(acc)
            if first_valid_row:
                rows = lax.broadcasted_iota(jnp.int32, (CONV_ROWS, 1), 0) + r0
                xact = jnp.where(rows >= first_valid_row, xact, 0.0)
            proj_ref[r0:r0 + CONV_ROWS, XBC_OFF + cols.start:XBC_OFF + cols.stop] = xact.astype(BF16)

    if with_attention:
        n_blocks = tm // BLOCK
        n_lanes = ATT_GROUP * BLOCK
        key = lax.broadcasted_iota(jnp.int32, (BLOCK, n_lanes), 0)
        qry = lax.broadcasted_iota(jnp.int32, (BLOCK, n_lanes), 1) % BLOCK
        in_cur = key <= qry
        rel = jnp.where(in_cur, qry - key, qry - key + BLOCK).astype(F32)
        rel_first = jnp.where(jnp.logical_or(in_cur, jnp.logical_not(first_tile)), rel, jnp.inf)
        cur_b = in_cur.astype(BF16)
        prev_b = 1.0 - cur_b
        low_lanes = lax.broadcasted_iota(jnp.int32, (1, LANES), 1) < ATT_HEAD_DIM
        low_rows = lax.broadcasted_iota(jnp.int32, (LANES, 1), 0) < ATT_HEAD_DIM
        nt = (((1,), (1,)), ((), ()))

        @pl.when(first_tile)
        def _():
            kd_scr[0] = jnp.zeros(kd_scr.shape[1:], BF16)
            vtd_scr[0] = jnp.zeros(vtd_scr.shape[1:], BF16)

        @pl.when(jnp.logical_not(first_tile))
        def _():
            kd_scr[0] = kd_scr[n_blocks]
            vtd_scr[0] = vtd_scr[n_blocks]

    def arrange_kv(blk):
        rows = slice(blk * BLOCK, (blk + 1) * BLOCK)
        k_f = proj_ref[rows, K_BLK * KV_WIDTH:(K_BLK + 1) * KV_WIDTH].astype(F32)
        v_f = proj_ref[rows, V_BLK * KV_WIDTH:(V_BLK + 1) * KV_WIDTH].astype(F32)
        for t in range(KV_WIDTH // LANES):
            k_tile = k_f[:, t * LANES:(t + 1) * LANES]
            k_swap = pltpu.roll(k_tile, ATT_HEAD_DIM, axis=1)
            kd_scr[blk + 1, 2 * t] = jnp.where(low_lanes, k_tile, k_swap).astype(BF16)
            kd_scr[blk + 1, 2 * t + 1] = jnp.where(low_lanes, k_swap, k_tile).astype(BF16)
            vt = v_f[:, t * LANES:(t + 1) * LANES].T
            vtd_scr[blk + 1, 2 * t] = jnp.concatenate([vt[:ATT_HEAD_DIM]] * 2, axis=0).astype(BF16)
            vtd_scr[blk + 1, 2 * t + 1] = jnp.concatenate([vt[ATT_HEAD_DIM:]] * 2, axis=0).astype(BF16)

    def attend(blk, g):
        rows = slice(blk * BLOCK, (blk + 1) * BLOCK)
        k_dup = jnp.concatenate([kd_scr[blk, g], kd_scr[blk + 1, g]], axis=0)
        vt_dup = jnp.concatenate([vtd_scr[blk, g], vtd_scr[blk + 1, g]], axis=1)
        q_rows = []
        for t in (2 * g, 2 * g + 1):
            q_tile = proj_ref[rows, Q_BLK * ATT_WIDTH + t * LANES:Q_BLK * ATT_WIDTH + (t + 1) * LANES]
            q_rows += [jnp.where(low_lanes, q_tile, 0), jnp.where(low_lanes, 0, q_tile)]
        q_cat = jnp.concatenate(q_rows, axis=0)
        s_band = lax.dot_general(k_dup, q_cat, nt, preferred_element_type=F32)
        s_meta = lax.dot_general(km_ref[g], q_cat, nt, preferred_element_type=F32)

        heads = range(g * ATT_GROUP, (g + 1) * ATT_GROUP)
        slopes = jnp.concatenate(
            [jnp.full((1, BLOCK), 2.0 ** (-8.0 * (h + 1) / ATT_Q_HEADS), F32) for h in heads], axis=1)
        sinks = jnp.concatenate([jnp.full((1, BLOCK), sink_ref[h], F32) for h in heads], axis=1)
        s = jnp.where(in_cur, s_band[BLOCK:], s_band[:BLOCK]) - slopes * (rel_first if blk == 0 else rel)
        m = jnp.maximum(jnp.max(s, axis=0, keepdims=True), jnp.max(s_meta, axis=0, keepdims=True))
        m = jnp.maximum(m, sinks)
        p = jnp.exp(s - m)
        p_meta = jnp.exp(s_meta - m)
        denom = (jnp.sum(p, axis=0, keepdims=True) + jnp.sum(p_meta, axis=0, keepdims=True)
                 + jnp.exp(sinks - m))
        p_b = p.astype(BF16)
        p_band = jnp.concatenate([p_b * prev_b, p_b * cur_b], axis=0)
        o_t = jnp.dot(vt_dup, p_band, preferred_element_type=F32)
        o_t = o_t + jnp.dot(vmt_ref[g], p_meta.astype(BF16), preferred_element_type=F32)
        o_t = o_t * (1.0 / denom)
        for pair in range(ATT_GROUP // 2):
            head_a = o_t[:, (2 * pair) * BLOCK:(2 * pair + 1) * BLOCK]
            head_b = o_t[:, (2 * pair + 1) * BLOCK:(2 * pair + 2) * BLOCK]
            tile = 2 * g + pair
            o_ref[rows, tile * LANES:(tile + 1) * LANES] = jnp.where(low_rows, head_a, head_b).T.astype(BF16)

    early, late = _plain_chunks(w_ref, wgate_ref)
    plain = early + late
    n_xbc = CONV_DIM // PROJ_CHUNK
    n_items = n_xbc + len(plain)
    xbc_at = {k * n_items // n_xbc: k for k in range(n_xbc)}
    plain_iter = iter(plain)
    matmuls = [xbc_at[n] if n in xbc_at else next(plain_iter) for n in range(n_items)]
    conv_per_matmul = -(-n_xbc * (tm // CONV_ROWS) // (len(matmuls) - 1))
    n_attn = (tm // BLOCK) * ATT_KV_HEADS if with_attention else 0
    attn_start = matmuls.index(early[-1]) + 1
    attn_per_matmul = -(-n_attn // (len(matmuls) - attn_start))
    attn_pending = []
    conv_pending = []
    for n, item in enumerate(matmuls):
        if isinstance(item, int):
            assert all(k > item - 2 for k, _ in conv_pending), "raw_scr slot still in use"
            xbc_matmul(item)
            conv_pending += [(item, r0) for r0 in range(0, tm, CONV_ROWS)]
        else:
            plain_chunk(*item)
        if with_attention and n + 1 == attn_start:
            for blk in range(n_blocks):
                arrange_kv(blk)
            attn_pending = [(blk, g) for blk in range(n_blocks) for g in range(ATT_KV_HEADS)]
        for piece in conv_pending[:conv_per_matmul]:
            conv_piece(*piece)
        conv_pending = conv_pending[conv_per_matmul:]
        for piece in attn_pending[:attn_per_matmul]:
            attend(*piece)
        attn_pending = attn_pending[attn_per_matmul:]
    for piece in conv_pending:
        conv_piece(*piece)
    for piece in attn_pending:
        attend(*piece)


def _plain_chunks(w_ref, wgate_ref):
    def chunks(ref, src, dst, width):
        return [(ref, src + c, dst + c) for c in range(0, width, PROJ_CHUNK)]

    early = chunks(w_ref, W_Q_OFF, Q_BLK * ATT_WIDTH, ATT_WIDTH) + chunks(w_ref, W_KV_OFF, K_BLK * KV_WIDTH, 2 * KV_WIDTH)
    late = (chunks(w_ref, W_ZATT_OFF, ZATT_BLK * ATT_WIDTH, ATT_WIDTH)
            + chunks(w_ref, W_ZSSM_OFF, ZSSM_BLK * SSM_INNER, SSM_INNER)
            + chunks(wgate_ref, 0, GATE_BLK * 2 * D_MODEL, 2 * D_MODEL))
    return early, late


def _inproj(h2, g_pre, w_main, w_gate, w_dt, conv_w, conv_b, tail0, attention=None, *, tm, tiles_per_seq,
            first_valid_row, emit_tail):
    rows = h2.shape[0]
    assert tm % CONV_ROWS == 0 and tm % BLOCK == 0
    operands = [h2, g_pre, w_main, w_gate, w_dt, conv_w, conv_b, tail0]
    in_specs = [
        pl.BlockSpec((tm, D_MODEL), lambda i: (i, 0)),
        _const_spec((1, D_MODEL), 1),
        _const_spec((D_MODEL, W_MAIN_WIDTH), 1),
        _const_spec((D_MODEL, 2 * D_MODEL), 1),
        _const_spec((D_MODEL, LANES), 1),
        _const_spec((CONV_WIDTH, CONV_DIM), 1),
        _const_spec((1, CONV_DIM), 1),
        _const_spec((SUBLANES, CONV_DIM), 1),
    ]
    out_specs = [pl.BlockSpec((tm, PACK_WIDTH), lambda i: (i, 0)), pl.BlockSpec((tm, LANES), lambda i: (i, 0))]
    out_shape = [jax.ShapeDtypeStruct((rows, PACK_WIDTH), BF16), jax.ShapeDtypeStruct((rows, LANES), F32)]
    scratch = [pltpu.VMEM((SUBLANES, CONV_DIM), F32),
               pltpu.VMEM((2, SUBLANES + tm, PROJ_CHUNK), F32)]
    if emit_tail:
        assert rows == tm
        out_specs.append(pl.BlockSpec((SUBLANES, CONV_DIM), lambda i: (0, 0)))
        out_shape.append(jax.ShapeDtypeStruct((SUBLANES, CONV_DIM), F32))
    if attention is not None:
        operands += list(attention)
        in_specs += [pl.BlockSpec(memory_space=pltpu.SMEM),
                     _const_spec((ATT_KV_HEADS, N_META, LANES), 1),
                     _const_spec((ATT_KV_HEADS, LANES, N_META), 1)]
        out_specs.append(pl.BlockSpec((tm, ATT_WIDTH), lambda i: (i, 0)))
        out_shape.append(jax.ShapeDtypeStruct((rows, ATT_WIDTH), BF16))
        kv_slots = (tm // BLOCK + 1, ATT_KV_HEADS, BLOCK, LANES)
        scratch += [pltpu.VMEM(kv_slots, BF16), pltpu.VMEM(kv_slots, BF16)]
    return pl.pallas_call(
        functools.partial(_inproj_kernel, tiles_per_seq=tiles_per_seq, first_valid_row=first_valid_row,
                          emit_tail=emit_tail, with_attention=attention is not None),
        grid=(rows // tm,),
        in_specs=in_specs,
        out_specs=out_specs,
        out_shape=out_shape,
        scratch_shapes=scratch,
        compiler_params=pltpu.CompilerParams(dimension_semantics=("arbitrary",),
                                             vmem_limit_bytes=VMEM_LIMIT_IN_PROJ),
        name="in_proj",
    )(*operands)


def _meta_kv(proj_meta):
    k_off, v_off = K_BLK * KV_WIDTH, V_BLK * KV_WIDTH
    k_dup, vt_dup = [], []
    for g in range(ATT_KV_HEADS):
        kg = proj_meta[PAD:, k_off + g * ATT_HEAD_DIM:k_off + (g + 1) * ATT_HEAD_DIM]
        vg = proj_meta[PAD:, v_off + g * ATT_HEAD_DIM:v_off + (g + 1) * ATT_HEAD_DIM]
        k_dup.append(jnp.concatenate([kg, kg], axis=1))
        vt_dup.append(jnp.concatenate([vg.T, vg.T], axis=0))
    return jnp.stack(k_dup), jnp.stack(vt_dup)


def _split3(a):
    hi = a.astype(BF16).astype(F32)
    r1 = a - hi
    mid = r1.astype(BF16).astype(F32)
    return hi, mid, r1 - mid


def _expand_lhs(x):
    hi, mid, lo = _split3(x)
    return (hi + pltpu.roll(mid, SSM_HEADS, axis=1) + pltpu.roll(lo, 2 * SSM_HEADS, axis=1)).astype(BF16)


def _expand_matrix():
    r = np.arange(LANES)[:, None]
    c = np.arange(SSM_INNER)[None, :]
    return jnp.asarray((r < 3 * SSM_HEADS) & (r % SSM_HEADS == c // SSM_HEAD_DIM), BF16)


def _ssd_chunk_fns(xact_ref, dt_ref, dtb_ref, alog_ref, dskip_ref, expand_ref, state_scr, q2row_scr, y_ref):
    emit_y = y_ref is not None
    row = lax.broadcasted_iota(jnp.int32, (BLOCK, BLOCK), 0)
    col = lax.broadcasted_iota(jnp.int32, (BLOCK, BLOCK), 1)
    tril = row >= col
    tril_b = tril.astype(BF16)
    lane = lax.broadcasted_iota(jnp.int32, (1, LANES), 1)
    head_lane = lane < SSM_HEADS
    low_lane = lane < SSM_HEAD_DIM
    low_b = low_lane.astype(BF16)
    high_b = 1.0 - low_b
    neg_a2 = -jnp.exp(alog_ref[...]) * LOG2E
    nt = (((1,), (1,)), ((), ()))
    n_bc = SSM_GROUPS * SSM_STATE

    def decays(ch):
        rows = slice(ch * BLOCK, (ch + 1) * BLOCK)
        x_dt = dt_ref[rows, :] + dtb_ref[...]
        dt = jnp.maximum(x_dt, 0.0) + jnp.log1p(jnp.exp(-jnp.abs(x_dt)))
        a2 = dt * neg_a2
        cs3 = jnp.dot(tril_b, jnp.concatenate(_split3(a2), axis=1).astype(BF16), preferred_element_type=F32)
        cs2 = cs3[:, :LANES] + cs3[:, LANES:2 * LANES] + cs3[:, 2 * LANES:]
        q2 = cs2 - jnp.log(dt) * LOG2E
        cs2_last = cs2[BLOCK - 1:BLOCK, :]
        q2row_scr[ch] = q2.T
        w = jnp.where(head_lane, jnp.exp2(cs2_last - q2), 0.0)
        cd = jnp.where(head_lane, jnp.exp2(cs2_last), 0.0)
        parts = [_expand_lhs(w), _expand_lhs(jnp.broadcast_to(cd, (2 * SUBLANES, LANES)))]
        ex = jnp.dot(jnp.concatenate(parts, axis=0), expand_ref[...], preferred_element_type=F32)
        return cs2, ex[:BLOCK], ex[BLOCK:BLOCK + 1]

    def group(ch, g, cs2, w_exp, cd_exp):
        rows = slice(ch * BLOCK, (ch + 1) * BLOCK)
        gcols = slice(g * SSM_GROUP_WIDTH, (g + 1) * SSM_GROUP_WIDTH)
        bg = xact_ref[rows,SSM_INNER + g * SSM_STATE:SSM_INNER + (g + 1) * SSM_STATE]
        cg = xact_ref[rows,SSM_INNER + n_bc + g * SSM_STATE:SSM_INNER + n_bc + (g + 1) * SSM_STATE]
        xs_f = xact_ref[rows,gcols].astype(F32)
        s_prev = state_scr[:, gcols]
        bg_t = bg.astype(F32).T.astype(BF16)
        new_states = jnp.dot(bg_t, (xs_f * w_exp[:, gcols]).astype(BF16), preferred_element_type=F32)
        state_scr[:, gcols] = s_prev * cd_exp[:, gcols] + new_states
        if not emit_y:
            return
        cb = lax.dot_general(cg, bg, nt, preferred_element_type=F32)
        y_off = jnp.dot(cg, s_prev.astype(BF16), preferred_element_type=F32)
        for pair in range(SSM_HEADS_PER_GROUP // 2):
            h0 = g * SSM_HEADS_PER_GROUP + 2 * pair
            cols = slice(h0 * SSM_HEAD_DIM, (h0 + 2) * SSM_HEAD_DIM)
            pcols = slice(2 * pair * SSM_HEAD_DIM, (2 * pair + 2) * SSM_HEAD_DIM)
            m_parts, from_start = [], []
            for h in (h0, h0 + 1):
                cs_l = jnp.broadcast_to(cs2[:, h:h + 1], (BLOCK, BLOCK))
                expo = jnp.where(tril, cs_l - q2row_scr[ch, h:h + 1, :], -jnp.inf)
                m_parts.append((cb * jnp.exp2(expo)).astype(BF16))
                from_start.append(jnp.exp2(cs_l))
            e_pair = jnp.where(low_lane, from_start[0], from_start[1])
            x_pair = xact_ref[rows,cols]
            x_bd = jnp.concatenate([x_pair * low_b, x_pair * high_b], axis=0)
            y_pair = jnp.dot(jnp.concatenate(m_parts, axis=1), x_bd, preferred_element_type=F32)
            y_pair = y_pair + e_pair * y_off[:, pcols] + dskip_ref[:, cols] * xs_f[:, pcols]
            y_ref[rows, cols] = y_pair.astype(y_ref.dtype)

    return decays, group


def _ssd_state_kernel(xact_ref, dt_ref, dtb_ref, alog_ref, dskip_ref, expand_ref, sout_ref, q2row_scr):
    sout_ref[...] = jnp.zeros(sout_ref.shape, F32)
    decays, group = _ssd_chunk_fns(xact_ref, dt_ref, dtb_ref, alog_ref, dskip_ref, expand_ref, sout_ref,
                                   q2row_scr, None)
    terms = decays(0)
    for g in range(SSM_GROUPS):
        group(0, g, *terms)


def _ssd_leading_state(proj_m, dt_m, dt_bias, a_log, d_skip):
    return pl.pallas_call(
        _ssd_state_kernel,
        grid=(1,),
        in_specs=[
            pl.BlockSpec((BLOCK, CONV_DIM), lambda i: (0, XBC_BLK)),
            _const_spec((BLOCK, LANES), 1),
            _const_spec((1, LANES), 1),
            _const_spec((1, LANES), 1),
            _const_spec((1, SSM_INNER), 1),
            _const_spec((LANES, SSM_INNER), 1),
        ],
        out_specs=pl.BlockSpec((SSM_STATE, SSM_INNER), lambda i: (0, 0)),
        out_shape=jax.ShapeDtypeStruct((SSM_STATE, SSM_INNER), F32),
        scratch_shapes=[pltpu.VMEM((1, LANES, BLOCK), F32)],
        compiler_params=pltpu.CompilerParams(vmem_limit_bytes=VMEM_LIMIT),
        name="ssd_leading_state",
    )(proj_m, dt_m, dt_bias, a_log, d_skip, _expand_matrix())


def _ssd_merge_kernel(xact_ref, dt_ref, zssm_ref, dtb_ref, alog_ref, dskip_ref, expand_ref, s0_ref, gnorm_ref,
                      o_ref, zatt_ref, gate_ref, x_ref, woa_ref, wos_ref, wo_ref, gpost_ref,
                      out_ref, state_scr, q2row_scr, y_scr, ssm_scr, att_scr, ya_scr, merged_scr,
                      *, tiles_per_seq, n_tiles):
    tm = x_ref.shape[0]
    i = pl.program_id(0)
    scan_tile = jnp.minimum(i, n_tiles - 1)
    scan_slot = i % 2
    merge_slot = (i + 1) % 2

    @pl.when(scan_tile % tiles_per_seq == 0)
    def _():
        state_scr[...] = s0_ref[...]

    @pl.when(i == 0)
    def _():
        ssm_scr[1] = jnp.zeros(ssm_scr.shape[1:], BF16)

    decays, group = _ssd_chunk_fns(xact_ref, dt_ref, dtb_ref, alog_ref, dskip_ref, expand_ref, state_scr,
                                   q2row_scr, y_scr)

    def branch_piece(ch):
        rows = slice(ch * BLOCK, (ch + 1) * BLOCK)
        y = y_scr[rows, :] * zssm_ref[rows, :].astype(F32)
        for g in range(SSM_GROUPS):
            cols = slice(g * SSM_GROUP_WIDTH, (g + 1) * SSM_GROUP_WIDTH)
            yg = y[:, cols]
            yg = yg * lax.rsqrt(jnp.mean(yg * yg, axis=-1, keepdims=True) + NORM_EPS)
            ssm_scr[scan_slot, rows, cols] = (yg * gnorm_ref[:, cols]).astype(BF16)

    def att_piece(r0):
        rows = slice(r0, r0 + PIECE_ROWS)
        att_scr[rows, :] = (o_ref[rows, :].astype(F32) * zatt_ref[rows, :].astype(F32)).astype(BF16)

    def att_chunk(c0):
        cols = slice(c0, c0 + OUT_CHUNK)
        ya_scr[:, cols] = jnp.dot(att_scr[...], woa_ref[:, cols], preferred_element_type=F32)

    def ssm_merge_chunk(c0):
        cols = slice(c0, c0 + OUT_CHUNK)
        ys = jnp.dot(ssm_scr[merge_slot], wos_ref[:, cols], preferred_element_type=F32)
        gate_att = gate_ref[:, cols].astype(F32)
        gate_ssm = gate_ref[:, D_MODEL + c0:D_MODEL + c0 + OUT_CHUNK].astype(F32)
        merged_scr[:, cols] = (gate_att * ya_scr[:, cols] + gate_ssm * ys).astype(BF16)

    def out_chunk(c0):
        cols = slice(c0, c0 + OUT_CHUNK)
        out_ref[:, cols] = jnp.dot(merged_scr[...], wo_ref[:, cols], preferred_element_type=F32)

    def finish():
        out = out_ref[...]
        ms = jnp.mean(out * out, axis=-1, keepdims=True)
        out_ref[...] = x_ref[...] + out * lax.rsqrt(ms + NORM_EPS) * gpost_ref[...]

    chunks = range(0, D_MODEL, OUT_CHUNK)
    pieces = [functools.partial(att_piece, r0) for r0 in range(0, tm, PIECE_ROWS)]
    matmuls = ([functools.partial(att_chunk, c0) for c0 in chunks]
               + [functools.partial(ssm_merge_chunk, c0) for c0 in chunks]
               + [functools.partial(out_chunk, c0) for c0 in chunks] + [finish])

    n_chunks = tm // BLOCK
    n_slots = n_chunks * SSM_GROUPS
    lead = 2
    per_lead = -(-len(pieces) // lead)
    per_slot = -(-len(matmuls) // (n_slots - lead))
    terms = decays(0)
    slot = 0
    for ch in range(n_chunks):
        next_terms = None
        for g in range(SSM_GROUPS):
            group(ch, g, *terms)
            if g == SSM_GROUPS // 2 - 1 and ch + 1 < n_chunks:
                next_terms = decays(ch + 1)
            if slot < lead:
                todo, pieces = pieces[:per_lead], pieces[per_lead:]
            else:
                todo, matmuls = matmuls[:per_slot], matmuls[per_slot:]
            for piece in todo:
                piece()
            slot += 1
        branch_piece(ch)
        terms = next_terms
    for piece in pieces + matmuls:
        piece()


def _ssd_merge(proj, dt, o_att, x2, dt_bias, a_log, d_skip, s0, g_norm, w_out_att, w_out_ssm, w_out, g_post,
               *, tm, tiles_per_seq):
    rows = x2.shape[0]
    n_tiles = rows // tm
    scan = lambda col: (lambda i: (jnp.minimum(i, n_tiles - 1), col))
    merge = lambda col: (lambda i: (jnp.maximum(i - 1, 0), col))
    return pl.pallas_call(
        functools.partial(_ssd_merge_kernel, tiles_per_seq=tiles_per_seq, n_tiles=n_tiles),
        grid=(n_tiles + 1,),
        in_specs=[
            pl.BlockSpec((tm, CONV_DIM), scan(XBC_BLK)),
            pl.BlockSpec((tm, LANES), scan(0)),
            pl.BlockSpec((tm, SSM_INNER), scan(ZSSM_BLK)),
            _const_spec((1, LANES), 1),
            _const_spec((1, LANES), 1),
            _const_spec((1, SSM_INNER), 1),
            _const_spec((LANES, SSM_INNER), 1),
            _const_spec((SSM_STATE, SSM_INNER), 1),
            _const_spec((1, SSM_INNER), 1),
            pl.BlockSpec((tm, ATT_WIDTH), merge(0)),
            pl.BlockSpec((tm, ATT_WIDTH), merge(ZATT_BLK)),
            pl.BlockSpec((tm, 2 * D_MODEL), merge(GATE_BLK)),
            pl.BlockSpec((tm, D_MODEL), merge(0)),
            _const_spec((ATT_WIDTH, D_MODEL), 1),
            _const_spec((SSM_INNER, D_MODEL), 1),
            _const_spec((D_MODEL, D_MODEL), 1),
            _const_spec((1, D_MODEL), 1),
        ],
        out_specs=pl.BlockSpec((tm, D_MODEL), merge(0)),
        out_shape=jax.ShapeDtypeStruct((rows, D_MODEL), F32),
        scratch_shapes=[
            pltpu.VMEM((SSM_STATE, SSM_INNER), F32),
            pltpu.VMEM((tm // BLOCK, LANES, BLOCK), F32),
            pltpu.VMEM((tm, SSM_INNER), F32),
            pltpu.VMEM((2, tm, SSM_INNER), BF16),
            pltpu.VMEM((tm, ATT_WIDTH), BF16),
            pltpu.VMEM((tm, D_MODEL), F32),
            pltpu.VMEM((tm, D_MODEL), BF16),
        ],
        compiler_params=pltpu.CompilerParams(dimension_semantics=("arbitrary",),
                                             vmem_limit_bytes=VMEM_LIMIT_SSD_MERGE),
        name="ssd_merge",
    )(proj, dt, proj, dt_bias, a_log, d_skip, _expand_matrix(), s0, g_norm,
      o_att, proj, proj, x2, w_out_att, w_out_ssm, w_out, g_post)


def _transpose_cast_kernel(wt_ref, o_ref):
    o_ref[...] = wt_ref[...].T.astype(o_ref.dtype)


def _transpose_cast(w_t, n_features, *, block):
    assert n_features % block == 0 and block % LANES == 0
    return pl.pallas_call(
        _transpose_cast_kernel,
        grid=(n_features // block,),
        in_specs=[pl.BlockSpec((block, D_MODEL), lambda j: (j, 0))],
        out_specs=pl.BlockSpec((D_MODEL, block), lambda j: (0, j)),
        out_shape=jax.ShapeDtypeStruct((D_MODEL, n_features), BF16),
        compiler_params=pltpu.CompilerParams(vmem_limit_bytes=VMEM_LIMIT),
        name="transpose_cast_weight",
    )(w_t)


def _in_proj_weights(w_in):
    assert w_in.shape[1] == W_GATE_OFF + 2 * D_MODEL == sum(SPLIT_SIZES)
    w_t = w_in.T
    w_main = _transpose_cast(w_t, W_MAIN_WIDTH, block=W_MAIN_WIDTH // 5)
    n_tail = SSM_HEADS + 2 * D_MODEL
    w_gate, w_dt = pl.pallas_call(
        _tail_weights_kernel,
        grid=(1,),
        in_specs=[pl.BlockSpec((pl.Element(n_tail), pl.Element(D_MODEL)), lambda j: (W_DT_OFF, 0))],
        out_specs=[pl.BlockSpec((D_MODEL, 2 * D_MODEL), lambda j: (0, 0)),
                   pl.BlockSpec((D_MODEL, LANES), lambda j: (0, 0))],
        out_shape=[jax.ShapeDtypeStruct((D_MODEL, 2 * D_MODEL), BF16),
                   jax.ShapeDtypeStruct((D_MODEL, LANES), BF16)],
        compiler_params=pltpu.CompilerParams(vmem_limit_bytes=VMEM_LIMIT),
        name="transpose_cast_tail",
    )(w_t)
    return w_main, w_gate, w_dt


def _tail_weights_kernel(wt_ref, gate_ref, dt_ref):
    gate_ref[...] = wt_ref[SSM_HEADS:, :].T.astype(BF16)
    dt_cols = wt_ref[:LANES, :].T
    lane = lax.broadcasted_iota(jnp.int32, (1, LANES), 1)
    dt_ref[...] = jnp.where(lane < SSM_HEADS, dt_cols, 0.0).astype(BF16)


def _pad_lanes(v):
    return jnp.pad(v.astype(F32), (0, LANES - v.shape[0])).reshape(1, LANES)


def kernel(x, meta_tokens, g_pre, w_in, conv_w, conv_b, dt_bias, a_log, d_skip, attn_sinks, g_ssm_norm,
           w_out_att, w_out_ssm, w_out, g_post):
    b, seq, d = x.shape
    assert d == D_MODEL and PAD + N_META == BLOCK
    assert g_pre.shape[0] == 1, "one layer"
    in_tile, out_tile = 512, 512
    assert seq % in_tile == 0 and seq % out_tile == 0

    in_weights = _in_proj_weights(w_in[0])
    g_pre2 = g_pre[0].reshape(1, D_MODEL)
    conv_b2 = conv_b[0].reshape(1, CONV_DIM)
    ssd_params = (_pad_lanes(dt_bias[0]), _pad_lanes(a_log[0]),
                  jnp.repeat(d_skip[0].astype(F32), SSM_HEAD_DIM).reshape(1, SSM_INNER))

    h_meta = jnp.concatenate([jnp.zeros((PAD, D_MODEL), x.dtype), meta_tokens.astype(x.dtype)], axis=0)
    proj_m, dt_m, tail_m = _inproj(h_meta, g_pre2, *in_weights, conv_w[0], conv_b2,
                                   jnp.zeros((SUBLANES, CONV_DIM), F32),
                                   tm=BLOCK, tiles_per_seq=1, first_valid_row=PAD, emit_tail=True)
    state_m = _ssd_leading_state(proj_m, dt_m, *ssd_params)

    x2 = x.reshape(b * seq, D_MODEL)
    proj, dt, o_att = _inproj(x2, g_pre2, *in_weights, conv_w[0], conv_b2, tail_m,
                              (attn_sinks[0].astype(F32), *_meta_kv(proj_m)),
                              tm=in_tile, tiles_per_seq=seq // in_tile, first_valid_row=0, emit_tail=False)
    out = _ssd_merge(proj, dt, o_att, x2, *ssd_params, state_m, g_ssm_norm[0].reshape(1, SSM_INNER),
                     w_out_att[0].astype(BF16), w_out_ssm[0].astype(BF16), w_out[0].astype(BF16),
                     g_post[0].reshape(1, D_MODEL), tm=out_tile, tiles_per_seq=seq // out_tile)
    return out.reshape(b, seq, D_MODEL)
```

```python
import functools
import math

import numpy as np
import jax
import jax.numpy as jnp
from jax import lax
from jax.experimental import pallas as pl
from jax.experimental.pallas import tpu as pltpu

F32 = jnp.float32
BF16 = jnp.bfloat16

D_MODEL = 1024
N_META = 16
BLOCK = 128
PAD = (-N_META) % BLOCK
NORM_EPS = 1e-6

ATT_HEAD_DIM = 64
ATT_Q_HEADS = D_MODEL // ATT_HEAD_DIM
ATT_KV_HEADS = 4
ATT_GROUP = ATT_Q_HEADS // ATT_KV_HEADS
ATT_WIDTH = ATT_Q_HEADS * ATT_HEAD_DIM
KV_WIDTH = ATT_KV_HEADS * ATT_HEAD_DIM

SSM_INNER = 2 * D_MODEL
SSM_HEAD_DIM = 64
SSM_HEADS = SSM_INNER // SSM_HEAD_DIM
SSM_GROUPS = 4
SSM_HEADS_PER_GROUP = SSM_HEADS // SSM_GROUPS
SSM_GROUP_WIDTH = SSM_INNER // SSM_GROUPS
SSM_STATE = 128
CONV_WIDTH = 4
CONV_DIM = SSM_INNER + 2 * SSM_GROUPS * SSM_STATE

SPLIT_SIZES = (ATT_WIDTH, KV_WIDTH, KV_WIDTH, ATT_WIDTH, SSM_INNER, CONV_DIM, SSM_HEADS, D_MODEL, D_MODEL)

LANES = 128
SUBLANES = 8
LOG2E = math.log2(math.e)

PACK_WIDTH = ATT_WIDTH + ATT_WIDTH + SSM_INNER + 2 * D_MODEL + CONV_DIM + 2 * KV_WIDTH
Q_BLK = 0
ZATT_BLK = 1
ZSSM_BLK = 1
GATE_BLK = 2
XBC_BLK = 2
XBC_OFF = XBC_BLK * CONV_DIM
K_BLK = (PACK_WIDTH - 2 * KV_WIDTH) // KV_WIDTH
V_BLK = K_BLK + 1
PROJ_CHUNK = 256
CONV_ROWS = 64
OUT_CHUNK = 256
PIECE_ROWS = 64
W_Q_OFF = 0
W_KV_OFF = ATT_WIDTH
W_ZATT_OFF = W_KV_OFF + 2 * KV_WIDTH
W_ZSSM_OFF = W_ZATT_OFF + ATT_WIDTH
W_XBC_OFF = W_ZSSM_OFF + SSM_INNER
W_MAIN_WIDTH = W_XBC_OFF + CONV_DIM
W_DT_OFF = W_MAIN_WIDTH
W_GATE_OFF = W_DT_OFF + SSM_HEADS
VMEM_LIMIT = 48 * 1024 * 1024
VMEM_LIMIT_IN_PROJ = 58 * 1024 * 1024
VMEM_LIMIT_SSD_MERGE = 56 * 1024 * 1024


def _sigmoid(x):
    return 0.5 * jnp.tanh(0.5 * x) + 0.5


def _silu(x):
    half = 0.5 * x
    return half * jnp.tanh(half) + half


def _const_spec(shape, n_grid):
    zeros = (0,) * len(shape)
    index_map = (lambda i: zeros) if n_grid == 1 else (lambda i, j: zeros)
    return pl.BlockSpec(shape, index_map, pipeline_mode=pl.Buffered(1))


def _inproj_kernel(h_ref, g_ref, w_ref, wgate_ref, wdt_ref, convw_ref, convb_ref, tail0_ref, *rest,
                   tiles_per_seq, first_valid_row, emit_tail, with_attention):
    rest = list(rest)
    if with_attention:
        sink_ref, km_ref, vmt_ref = rest[:3]
        del rest[:3]
    proj_ref, dt_ref = rest[:2]
    del rest[:2]
    tailout_ref = rest.pop(0) if emit_tail else None
    o_ref = rest.pop(0) if with_attention else None
    tail_scr, raw_scr = rest[:2]
    if with_attention:
        kd_scr, vtd_scr = rest[2:]
    tm = h_ref.shape[0]
    i = pl.program_id(0)
    first_tile = i % tiles_per_seq == 0

    @pl.when(first_tile)
    def _():
        tail_scr[...] = tail0_ref[...]

    h = h_ref[...]
    ms = jnp.mean(h * h, axis=-1, keepdims=True)
    u = (h * lax.rsqrt(ms + NORM_EPS) * g_ref[...]).astype(BF16)
    dt_ref[...] = jnp.dot(u, wdt_ref[...], preferred_element_type=F32)

    def project(src_ref, c0):
        return jnp.dot(u, src_ref[:, c0:c0 + PROJ_CHUNK], preferred_element_type=F32)

    def plain_chunk(src_ref, src_c0, out_c0):
        r = project(src_ref, src_c0)
        if out_c0 < ZATT_BLK * ATT_WIDTH:
            r = r * (ATT_HEAD_DIM ** -0.5)
        elif out_c0 < GATE_BLK * 2 * D_MODEL:
            r = _silu(r)
        elif out_c0 < XBC_OFF:
            r = _sigmoid(r)
        proj_ref[:, out_c0:out_c0 + PROJ_CHUNK] = r.astype(BF16)

    def xbc_matmul(k):
        cols = slice(k * PROJ_CHUNK, (k + 1) * PROJ_CHUNK)
        raw = raw_scr.at[k % 2]
        raw[0:SUBLANES, :] = tail_scr[:, cols]
        raw[SUBLANES:, :] = project(w_ref, W_XBC_OFF + k * PROJ_CHUNK)
        new_tail = raw[tm:, :]
        tail_scr[:, cols] = new_tail
        if emit_tail:
            tailout_ref[:, cols] = new_tail

    def conv_piece(k, r0):
        raw = raw_scr.at[k % 2]
        for l0 in range(0, PROJ_CHUNK, LANES):
            cols = slice(k * PROJ_CHUNK + l0, k * PROJ_CHUNK + l0 + LANES)
            w0, w1, w2, w3 = (convw_ref[t:t + 1, cols] for t in range(CONV_WIDTH))
            ext = raw[r0:r0 + SUBLANES + CONV_ROWS, l0:l0 + LANES]
            ext1 = pltpu.roll(ext, 1, axis=0)
            pair_old = w1 * ext + w0 * ext1
            acc = (convb_ref[:, cols] + w3 * ext[SUBLANES:] + w2 * ext1[SUBLANES:]
                   + pltpu.roll(pair_old, 2, axis=0)[SUBLANES:])
            xact = _silu(acc)
            if first_valid_row:
                rows = lax.broadcasted_iota(jnp.int32, (CONV_ROWS, 1), 0) + r0
                xact = jnp.where(rows >= first_valid_row, xact, 0.0)
            proj_ref[r0:r0 + CONV_ROWS, XBC_OFF + cols.start:XBC_OFF + cols.stop] = xact.astype(BF16)

    if with_attention:
        n_blocks = tm // BLOCK
        n_lanes = ATT_GROUP * BLOCK
        key = lax.broadcasted_iota(jnp.int32, (BLOCK, n_lanes), 0)
        qry = lax.broadcasted_iota(jnp.int32, (BLOCK, n_lanes), 1) % BLOCK
        in_cur = key <= qry
        rel = jnp.where(in_cur, qry - key, qry - key + BLOCK).astype(F32)
        rel_first = jnp.where(jnp.logical_or(in_cur, jnp.logical_not(first_tile)), rel, jnp.inf)
        cur_b = in_cur.astype(BF16)
        prev_b = 1.0 - cur_b
        low_lanes = lax.broadcasted_iota(jnp.int32, (1, LANES), 1) < ATT_HEAD_DIM
        low_rows = lax.broadcasted_iota(jnp.int32, (LANES, 1), 0) < ATT_HEAD_DIM
        nt = (((1,), (1,)), ((), ()))

        @pl.when(first_tile)
        def _():
            kd_scr[0] = jnp.zeros(kd_scr.shape[1:], BF16)
            vtd_scr[0] = jnp.zeros(vtd_scr.shape[1:], BF16)

        @pl.when(jnp.logical_not(first_tile))
        def _():
            kd_scr[0] = kd_scr[n_blocks]
            vtd_scr[0] = vtd_scr[n_blocks]

    def arrange_kv(blk):
        rows = slice(blk * BLOCK, (blk + 1) * BLOCK)
        k_f = proj_ref[rows, K_BLK * KV_WIDTH:(K_BLK + 1) * KV_WIDTH].astype(F32)
        v_f = proj_ref[rows, V_BLK * KV_WIDTH:(V_BLK + 1) * KV_WIDTH].astype(F32)
        for t in range(KV_WIDTH // LANES):
            k_tile = k_f[:, t * LANES:(t + 1) * LANES]
            k_swap = pltpu.roll(k_tile, ATT_HEAD_DIM, axis=1)
            kd_scr[blk + 1, 2 * t] = jnp.where(low_lanes, k_tile, k_swap).astype(BF16)
            kd_scr[blk + 1, 2 * t + 1] = jnp.where(low_lanes, k_swap, k_tile).astype(BF16)
            vt = v_f[:, t * LANES:(t + 1) * LANES].T
            vtd_scr[blk + 1, 2 * t] = jnp.concatenate([vt[:ATT_HEAD_DIM]] * 2, axis=0).astype(BF16)
            vtd_scr[blk + 1, 2 * t + 1] = jnp.concatenate([vt[ATT_HEAD_DIM:]] * 2, axis=0).astype(BF16)

    def attend(blk, g):
        rows = slice(blk * BLOCK, (blk + 1) * BLOCK)
        k_dup = jnp.concatenate([kd_scr[blk, g], kd_scr[blk + 1, g]], axis=0)
        vt_dup = jnp.concatenate([vtd_scr[blk, g], vtd_scr[blk + 1, g]], axis=1)
        half = slice(0, 2 * BLOCK)
        rel_blk = (rel_first if blk == 0 else rel)[:, half]
        for pair in range(ATT_GROUP // 2):
            tile = 2 * g + pair
            q_tile = proj_ref[rows, Q_BLK * ATT_WIDTH + tile * LANES:Q_BLK * ATT_WIDTH + (tile + 1) * LANES]
            q_cat = jnp.concatenate([jnp.where(low_lanes, q_tile, 0), jnp.where(low_lanes, 0, q_tile)], axis=0)
            s_band = lax.dot_general(k_dup, q_cat, nt, preferred_element_type=F32)
            s_meta = lax.dot_general(km_ref[g], q_cat, nt, preferred_element_type=F32)
            heads = (2 * tile, 2 * tile + 1)
            slopes = jnp.concatenate(
                [jnp.full((1, BLOCK), 2.0 ** (-8.0 * (h + 1) / ATT_Q_HEADS), F32) for h in heads], axis=1)
            sinks = jnp.concatenate([jnp.full((1, BLOCK), sink_ref[h], F32) for h in heads], axis=1)
            s = jnp.where(in_cur[:, half], s_band[BLOCK:], s_band[:BLOCK]) - slopes * rel_blk
            m = jnp.maximum(jnp.max(s, axis=0, keepdims=True), jnp.max(s_meta, axis=0, keepdims=True))
            m = jnp.maximum(m, sinks)
            p = jnp.exp(s - m)
            p_meta = jnp.exp(s_meta - m)
            denom = (jnp.sum(p, axis=0, keepdims=True) + jnp.sum(p_meta, axis=0, keepdims=True)
                     + jnp.exp(sinks - m))
            p_b = p.astype(BF16)
            p_band = jnp.concatenate([p_b * prev_b[:, half], p_b * cur_b[:, half]], axis=0)
            o_t = jnp.dot(vt_dup, p_band, preferred_element_type=F32)
            o_t = o_t + jnp.dot(vmt_ref[g], p_meta.astype(BF16), preferred_element_type=F32)
            o_t = o_t * (1.0 / denom)
            both = jnp.where(low_rows, o_t[:, :BLOCK], o_t[:, BLOCK:])
            o_ref[rows, tile * LANES:(tile + 1) * LANES] = both.T.astype(BF16)

    early, late = _plain_chunks(w_ref, wgate_ref)
    for chunk in early:
        plain_chunk(*chunk)
    attn_pending = []
    if with_attention:
        for blk in range(n_blocks):
            arrange_kv(blk)
        attn_pending = [(blk, g) for blk in range(n_blocks) for g in range(ATT_KV_HEADS)]
    n_xbc = CONV_DIM // PROJ_CHUNK
    n_items = n_xbc + len(late)
    xbc_at = {k * n_items // n_xbc: k for k in range(n_xbc)}
    late_iter = iter(late)
    matmuls = [xbc_at[n] if n in xbc_at else next(late_iter) for n in range(n_items)]
    conv_per_matmul = -(-n_xbc * (tm // CONV_ROWS) // (len(matmuls) - 1))
    attn_per_matmul = -(-len(attn_pending) // len(matmuls))
    conv_pending = []
    for item in matmuls:
        if isinstance(item, int):
            assert all(k > item - 2 for k, _ in conv_pending), "raw_scr slot still in use"
            xbc_matmul(item)
            conv_pending += [(item, r0) for r0 in range(0, tm, CONV_ROWS)]
        else:
            plain_chunk(*item)
        for piece in conv_pending[:conv_per_matmul]:
            conv_piece(*piece)
        conv_pending = conv_pending[conv_per_matmul:]
        for piece in attn_pending[:attn_per_matmul]:
            attend(*piece)
        attn_pending = attn_pending[attn_per_matmul:]
    for piece in conv_pending:
        conv_piece(*piece)
    for piece in attn_pending:
        attend(*piece)


def _plain_chunks(w_ref, wgate_ref):
    def chunks(ref, src, dst, width):
        return [(ref, src + c, dst + c) for c in range(0, width, PROJ_CHUNK)]

    early = chunks(w_ref, W_Q_OFF, Q_BLK * ATT_WIDTH, ATT_WIDTH) + chunks(w_ref, W_KV_OFF, K_BLK * KV_WIDTH, 2 * KV_WIDTH)
    late = (chunks(w_ref, W_ZATT_OFF, ZATT_BLK * ATT_WIDTH, ATT_WIDTH)
            + chunks(w_ref, W_ZSSM_OFF, ZSSM_BLK * SSM_INNER, SSM_INNER)
            + chunks(wgate_ref, 0, GATE_BLK * 2 * D_MODEL, 2 * D_MODEL))
    return early, late


def _inproj(h2, g_pre, w_main, w_gate, w_dt, conv_w, conv_b, tail0, attention=None, *, tm, tiles_per_seq,
            first_valid_row, emit_tail):
    rows = h2.shape[0]
    assert tm % CONV_ROWS == 0 and tm % BLOCK == 0
    operands = [h2, g_pre, w_main, w_gate, w_dt, conv_w, conv_b, tail0]
    in_specs = [
        pl.BlockSpec((tm, D_MODEL), lambda i: (i, 0)),
        _const_spec((1, D_MODEL), 1),
        _const_spec((D_MODEL, W_MAIN_WIDTH), 1),
        _const_spec((D_MODEL, 2 * D_MODEL), 1),
        _const_spec((D_MODEL, LANES), 1),
        _const_spec((CONV_WIDTH, CONV_DIM), 1),
        _const_spec((1, CONV_DIM), 1),
        _const_spec((SUBLANES, CONV_DIM), 1),
    ]
    out_specs = [pl.BlockSpec((tm, PACK_WIDTH), lambda i: (i, 0)), pl.BlockSpec((tm, LANES), lambda i: (i, 0))]
    out_shape = [jax.ShapeDtypeStruct((rows, PACK_WIDTH), BF16), jax.ShapeDtypeStruct((rows, LANES), F32)]
    scratch = [pltpu.VMEM((SUBLANES, CONV_DIM), F32),
               pltpu.VMEM((2, SUBLANES + tm, PROJ_CHUNK), F32)]
    if emit_tail:
        assert rows == tm
        out_specs.append(pl.BlockSpec((SUBLANES, CONV_DIM), lambda i: (0, 0)))
        out_shape.append(jax.ShapeDtypeStruct((SUBLANES, CONV_DIM), F32))
    if attention is not None:
        operands += list(attention)
        in_specs += [pl.BlockSpec(memory_space=pltpu.SMEM),
                     _const_spec((ATT_KV_HEADS, N_META, LANES), 1),
                     _const_spec((ATT_KV_HEADS, LANES, N_META), 1)]
        out_specs.append(pl.BlockSpec((tm, ATT_WIDTH), lambda i: (i, 0)))
        out_shape.append(jax.ShapeDtypeStruct((rows, ATT_WIDTH), BF16))
        kv_slots = (tm // BLOCK + 1, ATT_KV_HEADS, BLOCK, LANES)
        scratch += [pltpu.VMEM(kv_slots, BF16), pltpu.VMEM(kv_slots, BF16)]
    return pl.pallas_call(
        functools.partial(_inproj_kernel, tiles_per_seq=tiles_per_seq, first_valid_row=first_valid_row,
                          emit_tail=emit_tail, with_attention=attention is not None),
        grid=(rows // tm,),
        in_specs=in_specs,
        out_specs=out_specs,
        out_shape=out_shape,
        scratch_shapes=scratch,
        compiler_params=pltpu.CompilerParams(dimension_semantics=("arbitrary",),
                                             vmem_limit_bytes=VMEM_LIMIT_IN_PROJ),
        name="in_proj",
    )(*operands)


def _meta_kv(proj_meta):
    k_off, v_off = K_BLK * KV_WIDTH, V_BLK * KV_WIDTH
    k_dup, vt_dup = [], []
    for g in range(ATT_KV_HEADS):
        kg = proj_meta[PAD:, k_off + g * ATT_HEAD_DIM:k_off + (g + 1) * ATT_HEAD_DIM]
        vg = proj_meta[PAD:, v_off + g * ATT_HEAD_DIM:v_off + (g + 1) * ATT_HEAD_DIM]
        k_dup.append(jnp.concatenate([kg, kg], axis=1))
        vt_dup.append(jnp.concatenate([vg.T, vg.T], axis=0))
    return jnp.stack(k_dup), jnp.stack(vt_dup)


def _split3(a):
    hi = a.astype(BF16).astype(F32)
    r1 = a - hi
    mid = r1.astype(BF16).astype(F32)
    return hi, mid, r1 - mid


def _expand_lhs(x):
    hi, mid, lo = _split3(x)
    return (hi + pltpu.roll(mid, SSM_HEADS, axis=1) + pltpu.roll(lo, 2 * SSM_HEADS, axis=1)).astype(BF16)


def _expand_matrix():
    r = np.arange(LANES)[:, None]
    c = np.arange(SSM_INNER)[None, :]
    return jnp.asarray((r < 3 * SSM_HEADS) & (r % SSM_HEADS == c // SSM_HEAD_DIM), BF16)


def _ssd_chunk_fns(xact_ref, dt_ref, dtb_ref, alog_ref, dskip_ref, expand_ref, state_scr, q2row_scr, y_ref):
    emit_y = y_ref is not None
    row = lax.broadcasted_iota(jnp.int32, (BLOCK, BLOCK), 0)
    col = lax.broadcasted_iota(jnp.int32, (BLOCK, BLOCK), 1)
    tril = row >= col
    tril_b = tril.astype(BF16)
    lane = lax.broadcasted_iota(jnp.int32, (1, LANES), 1)
    head_lane = lane < SSM_HEADS
    low_lane = lane < SSM_HEAD_DIM
    low_b = low_lane.astype(BF16)
    high_b = 1.0 - low_b
    neg_a2 = -jnp.exp(alog_ref[...]) * LOG2E
    nt = (((1,), (1,)), ((), ()))
    n_bc = SSM_GROUPS * SSM_STATE

    def decays(ch):
        rows = slice(ch * BLOCK, (ch + 1) * BLOCK)
        x_dt = dt_ref[rows, :] + dtb_ref[...]
        dt = jnp.maximum(x_dt, 0.0) + jnp.log1p(jnp.exp(-jnp.abs(x_dt)))
        a2 = dt * neg_a2
        cs3 = jnp.dot(tril_b, jnp.concatenate(_split3(a2), axis=1).astype(BF16), preferred_element_type=F32)
        cs2 = cs3[:, :LANES] + cs3[:, LANES:2 * LANES] + cs3[:, 2 * LANES:]
        q2 = cs2 - jnp.log(dt) * LOG2E
        cs2_last = cs2[BLOCK - 1:BLOCK, :]
        q2row_scr[ch] = q2.T
        w = jnp.where(head_lane, jnp.exp2(cs2_last - q2), 0.0)
        cd = jnp.where(head_lane, jnp.exp2(cs2_last), 0.0)
        parts = [_expand_lhs(w), _expand_lhs(jnp.broadcast_to(cd, (2 * SUBLANES, LANES)))]
        ex = jnp.dot(jnp.concatenate(parts, axis=0), expand_ref[...], preferred_element_type=F32)
        return cs2, ex[:BLOCK], ex[BLOCK:BLOCK + 1]

    def group(ch, g, cs2, w_exp, cd_exp):
        rows = slice(ch * BLOCK, (ch + 1) * BLOCK)
        gcols = slice(g * SSM_GROUP_WIDTH, (g + 1) * SSM_GROUP_WIDTH)
        bg = xact_ref[rows,SSM_INNER + g * SSM_STATE:SSM_INNER + (g + 1) * SSM_STATE]
        cg = xact_ref[rows,SSM_INNER + n_bc + g * SSM_STATE:SSM_INNER + n_bc + (g + 1) * SSM_STATE]
        xs_f = xact_ref[rows,gcols].astype(F32)
        s_prev = state_scr[:, gcols]
        bg_t = bg.astype(F32).T.astype(BF16)
        new_states = jnp.dot(bg_t, (xs_f * w_exp[:, gcols]).astype(BF16), preferred_element_type=F32)
        state_scr[:, gcols] = s_prev * cd_exp[:, gcols] + new_states
        if not emit_y:
            return
        cb = lax.dot_general(cg, bg, nt, preferred_element_type=F32)
        y_off = jnp.dot(cg, s_prev.astype(BF16), preferred_element_type=F32)
        for pair in range(SSM_HEADS_PER_GROUP // 2):
            h0 = g * SSM_HEADS_PER_GROUP + 2 * pair
            cols = slice(h0 * SSM_HEAD_DIM, (h0 + 2) * SSM_HEAD_DIM)
            pcols = slice(2 * pair * SSM_HEAD_DIM, (2 * pair + 2) * SSM_HEAD_DIM)
            m_parts, from_start = [], []
            for h in (h0, h0 + 1):
                cs_l = jnp.broadcast_to(cs2[:, h:h + 1], (BLOCK, BLOCK))
                expo = jnp.where(tril, cs_l - q2row_scr[ch, h:h + 1, :], -jnp.inf)
                m_parts.append((cb * jnp.exp2(expo)).astype(BF16))
                from_start.append(jnp.exp2(cs_l))
            e_pair = jnp.where(low_lane, from_start[0], from_start[1])
            x_pair = xact_ref[rows,cols]
            x_bd = jnp.concatenate([x_pair * low_b, x_pair * high_b], axis=0)
            y_pair = jnp.dot(jnp.concatenate(m_parts, axis=1), x_bd, preferred_element_type=F32)
            y_pair = y_pair + e_pair * y_off[:, pcols] + dskip_ref[:, cols] * xs_f[:, pcols]
            y_ref[rows, cols] = y_pair.astype(y_ref.dtype)

    return decays, group


def _ssd_state_kernel(xact_ref, dt_ref, dtb_ref, alog_ref, dskip_ref, expand_ref, sout_ref, q2row_scr):
    sout_ref[...] = jnp.zeros(sout_ref.shape, F32)
    decays, group = _ssd_chunk_fns(xact_ref, dt_ref, dtb_ref, alog_ref, dskip_ref, expand_ref, sout_ref,
                                   q2row_scr, None)
    terms = decays(0)
    for g in range(SSM_GROUPS):
        group(0, g, *terms)


def _ssd_leading_state(proj_m, dt_m, dt_bias, a_log, d_skip):
    return pl.pallas_call(
        _ssd_state_kernel,
        grid=(1,),
        in_specs=[
            pl.BlockSpec((BLOCK, CONV_DIM), lambda i: (0, XBC_BLK)),
            _const_spec((BLOCK, LANES), 1),
            _const_spec((1, LANES), 1),
            _const_spec((1, LANES), 1),
            _const_spec((1, SSM_INNER), 1),
            _const_spec((LANES, SSM_INNER), 1),
        ],
        out_specs=pl.BlockSpec((SSM_STATE, SSM_INNER), lambda i: (0, 0)),
        out_shape=jax.ShapeDtypeStruct((SSM_STATE, SSM_INNER), F32),
        scratch_shapes=[pltpu.VMEM((1, LANES, BLOCK), F32)],
        compiler_params=pltpu.CompilerParams(vmem_limit_bytes=VMEM_LIMIT),
        name="ssd_leading_state",
    )(proj_m, dt_m, dt_bias, a_log, d_skip, _expand_matrix())


def _ssd_merge_kernel(xact_ref, dt_ref, zssm_ref, dtb_ref, alog_ref, dskip_ref, expand_ref, s0_ref, gnorm_ref,
                      o_ref, zatt_ref, gate_ref, x_ref, woa_ref, wos_ref, wo_ref, gpost_ref,
                      out_ref, state_scr, q2row_scr, y_scr, ssm_scr, att_scr, ya_scr, merged_scr,
                      *, tiles_per_seq, n_tiles):
    tm = x_ref.shape[0]
    i = pl.program_id(0)
    scan_tile = jnp.minimum(i, n_tiles - 1)
    scan_slot = i % 2
    merge_slot = (i + 1) % 2

    @pl.when(scan_tile % tiles_per_seq == 0)
    def _():
        state_scr[...] = s0_ref[...]

    @pl.when(i == 0)
    def _():
        ssm_scr[1] = jnp.zeros(ssm_scr.shape[1:], BF16)

    decays, group = _ssd_chunk_fns(xact_ref, dt_ref, dtb_ref, alog_ref, dskip_ref, expand_ref, state_scr,
                                   q2row_scr, y_scr)

    def branch_piece(ch):
        rows = slice(ch * BLOCK, (ch + 1) * BLOCK)
        y = y_scr[rows, :] * zssm_ref[rows, :].astype(F32)
        for g in range(SSM_GROUPS):
            cols = slice(g * SSM_GROUP_WIDTH, (g + 1) * SSM_GROUP_WIDTH)
            yg = y[:, cols]
            yg = yg * lax.rsqrt(jnp.mean(yg * yg, axis=-1, keepdims=True) + NORM_EPS)
            ssm_scr[scan_slot, rows, cols] = (yg * gnorm_ref[:, cols]).astype(BF16)

    def att_piece(r0):
        rows = slice(r0, r0 + PIECE_ROWS)
        att_scr[rows, :] = (o_ref[rows, :].astype(F32) * zatt_ref[rows, :].astype(F32)).astype(BF16)

    def att_chunk(c0):
        cols = slice(c0, c0 + OUT_CHUNK)
        ya_scr[:, cols] = jnp.dot(att_scr[...], woa_ref[:, cols], preferred_element_type=F32)

    def ssm_merge_chunk(c0):
        cols = slice(c0, c0 + OUT_CHUNK)
        ys = jnp.dot(ssm_scr[merge_slot], wos_ref[:, cols], preferred_element_type=F32)
        gate_att = gate_ref[:, cols].astype(F32)
        gate_ssm = gate_ref[:, D_MODEL + c0:D_MODEL + c0 + OUT_CHUNK].astype(F32)
        merged_scr[:, cols] = (gate_att * ya_scr[:, cols] + gate_ssm * ys).astype(BF16)

    def out_chunk(c0):
        cols = slice(c0, c0 + OUT_CHUNK)
        out_ref[:, cols] = jnp.dot(merged_scr[...], wo_ref[:, cols], preferred_element_type=F32)

    def finish():
        out = out_ref[...]
        ms = jnp.mean(out * out, axis=-1, keepdims=True)
        out_ref[...] = x_ref[...] + out * lax.rsqrt(ms + NORM_EPS) * gpost_ref[...]

    chunks = range(0, D_MODEL, OUT_CHUNK)
    pieces = [functools.partial(att_piece, r0) for r0 in range(0, tm, PIECE_ROWS)]
    matmuls = ([functools.partial(att_chunk, c0) for c0 in chunks]
               + [functools.partial(ssm_merge_chunk, c0) for c0 in chunks]
               + [functools.partial(out_chunk, c0) for c0 in chunks] + [finish])

    n_chunks = tm // BLOCK
    n_slots = n_chunks * SSM_GROUPS
    lead = 2
    per_lead = -(-len(pieces) // lead)
    per_slot = -(-len(matmuls) // (n_slots - lead))
    terms = decays(0)
    slot = 0
    for ch in range(n_chunks):
        next_terms = None
        for g in range(SSM_GROUPS):
            group(ch, g, *terms)
            if g == SSM_GROUPS // 2 - 1 and ch + 1 < n_chunks:
                next_terms = decays(ch + 1)
            if slot < lead:
                todo, pieces = pieces[:per_lead], pieces[per_lead:]
            else:
                todo, matmuls = matmuls[:per_slot], matmuls[per_slot:]
            for piece in todo:
                piece()
            slot += 1
        branch_piece(ch)
        terms = next_terms
    for piece in pieces + matmuls:
        piece()


def _ssd_merge(proj, dt, o_att, x2, dt_bias, a_log, d_skip, s0, g_norm, w_out_att, w_out_ssm, w_out, g_post,
               *, tm, tiles_per_seq):
    rows = x2.shape[0]
    n_tiles = rows // tm
    scan = lambda col: (lambda i: (jnp.minimum(i, n_tiles - 1), col))
    merge = lambda col: (lambda i: (jnp.maximum(i - 1, 0), col))
    return pl.pallas_call(
        functools.partial(_ssd_merge_kernel, tiles_per_seq=tiles_per_seq, n_tiles=n_tiles),
        grid=(n_tiles + 1,),
        in_specs=[
            pl.BlockSpec((tm, CONV_DIM), scan(XBC_BLK)),
            pl.BlockSpec((tm, LANES), scan(0)),
            pl.BlockSpec((tm, SSM_INNER), scan(ZSSM_BLK)),
            _const_spec((1, LANES), 1),
            _const_spec((1, LANES), 1),
            _const_spec((1, SSM_INNER), 1),
            _const_spec((LANES, SSM_INNER), 1),
            _const_spec((SSM_STATE, SSM_INNER), 1),
            _const_spec((1, SSM_INNER), 1),
            pl.BlockSpec((tm, ATT_WIDTH), merge(0)),
            pl.BlockSpec((tm, ATT_WIDTH), merge(ZATT_BLK)),
            pl.BlockSpec((tm, 2 * D_MODEL), merge(GATE_BLK)),
            pl.BlockSpec((tm, D_MODEL), merge(0)),
            _const_spec((ATT_WIDTH, D_MODEL), 1),
            _const_spec((SSM_INNER, D_MODEL), 1),
            _const_spec((D_MODEL, D_MODEL), 1),
            _const_spec((1, D_MODEL), 1),
        ],
        out_specs=pl.BlockSpec((tm, D_MODEL), merge(0)),
        out_shape=jax.ShapeDtypeStruct((rows, D_MODEL), F32),
        scratch_shapes=[
            pltpu.VMEM((SSM_STATE, SSM_INNER), F32),
            pltpu.VMEM((tm // BLOCK, LANES, BLOCK), F32),
            pltpu.VMEM((tm, SSM_INNER), F32),
            pltpu.VMEM((2, tm, SSM_INNER), BF16),
            pltpu.VMEM((tm, ATT_WIDTH), BF16),
            pltpu.VMEM((tm, D_MODEL), F32),
            pltpu.VMEM((tm, D_MODEL), BF16),
        ],
        compiler_params=pltpu.CompilerParams(dimension_semantics=("arbitrary",),
                                             vmem_limit_bytes=VMEM_LIMIT_SSD_MERGE),
        name="ssd_merge",
    )(proj, dt, proj, dt_bias, a_log, d_skip, _expand_matrix(), s0, g_norm,
      o_att, proj, proj, x2, w_out_att, w_out_ssm, w_out, g_post)


def _transpose_cast_kernel(wt_ref, o_ref):
    o_ref[...] = wt_ref[...].T.astype(o_ref.dtype)


def _transpose_cast(w_t, n_features, *, block):
    assert n_features % block == 0 and block % LANES == 0
    return pl.pallas_call(
        _transpose_cast_kernel,
        grid=(n_features // block,),
        in_specs=[pl.BlockSpec((block, D_MODEL), lambda j: (j, 0))],
        out_specs=pl.BlockSpec((D_MODEL, block), lambda j: (0, j)),
        out_shape=jax.ShapeDtypeStruct((D_MODEL, n_features), BF16),
        compiler_params=pltpu.CompilerParams(vmem_limit_bytes=VMEM_LIMIT),
        name="transpose_cast_weight",
    )(w_t)


def _in_proj_weights(w_in):
    assert w_in.shape[1] == W_GATE_OFF + 2 * D_MODEL == sum(SPLIT_SIZES)
    w_t = w_in.T
    w_main = _transpose_cast(w_t, W_MAIN_WIDTH, block=W_MAIN_WIDTH // 5)
    n_tail = SSM_HEADS + 2 * D_MODEL
    w_gate, w_dt = pl.pallas_call(
        _tail_weights_kernel,
        grid=(1,),
        in_specs=[pl.BlockSpec((pl.Element(n_tail), pl.Element(D_MODEL)), lambda j: (W_DT_OFF, 0))],
        out_specs=[pl.BlockSpec((D_MODEL, 2 * D_MODEL), lambda j: (0, 0)),
                   pl.BlockSpec((D_MODEL, LANES), lambda j: (0, 0))],
        out_shape=[jax.ShapeDtypeStruct((D_MODEL, 2 * D_MODEL), BF16),
                   jax.ShapeDtypeStruct((D_MODEL, LANES), BF16)],
        compiler_params=pltpu.CompilerParams(vmem_limit_bytes=VMEM_LIMIT),
        name="transpose_cast_tail",
    )(w_t)
    return w_main, w_gate, w_dt


def _tail_weights_kernel(wt_ref, gate_ref, dt_ref):
    gate_ref[...] = wt_ref[SSM_HEADS:, :].T.astype(BF16)
    dt_cols = wt_ref[:LANES, :].T
    lane = lax.broadcasted_iota(jnp.int32, (1, LANES), 1)
    dt_ref[...] = jnp.where(lane < SSM_HEADS, dt_cols, 0.0).astype(BF16)


def _pad_lanes(v):
    return jnp.pad(v.astype(F32), (0, LANES - v.shape[0])).reshape(1, LANES)


def kernel(x, meta_tokens, g_pre, w_in, conv_w, conv_b, dt_bias, a_log, d_skip, attn_sinks, g_ssm_norm,
           w_out_att, w_out_ssm, w_out, g_post):
    b, seq, d = x.shape
    assert d == D_MODEL and PAD + N_META == BLOCK
    assert g_pre.shape[0] == 1, "one layer"
    in_tile, out_tile = 512, 512
    assert seq % in_tile == 0 and seq % out_tile == 0

    in_weights = _in_proj_weights(w_in[0])
    g_pre2 = g_pre[0].reshape(1, D_MODEL)
    conv_b2 = conv_b[0].reshape(1, CONV_DIM)
    ssd_params = (_pad_lanes(dt_bias[0]), _pad_lanes(a_log[0]),
                  jnp.repeat(d_skip[0].astype(F32), SSM_HEAD_DIM).reshape(1, SSM_INNER))

    h_meta = jnp.concatenate([jnp.zeros((PAD, D_MODEL), x.dtype), meta_tokens.astype(x.dtype)], axis=0)
    proj_m, dt_m, tail_m = _inproj(h_meta, g_pre2, *in_weights, conv_w[0], conv_b2,
                                   jnp.zeros((SUBLANES, CONV_DIM), F32),
                                   tm=BLOCK, tiles_per_seq=1, first_valid_row=PAD, emit_tail=True)
    state_m = _ssd_leading_state(proj_m, dt_m, *ssd_params)

    x2 = x.reshape(b * seq, D_MODEL)
    proj, dt, o_att = _inproj(x2, g_pre2, *in_weights, conv_w[0], conv_b2, tail_m,
                              (attn_sinks[0].astype(F32), *_meta_kv(proj_m)),
                              tm=in_tile, tiles_per_seq=seq // in_tile, first_valid_row=0, emit_tail=False)
    out = _ssd_merge(proj, dt, o_att, x2, *ssd_params, state_m, g_ssm_norm[0].reshape(1, SSM_INNER),
                     w_out_att[0].astype(BF16), w_out_ssm[0].astype(BF16), w_out[0].astype(BF16),
                     g_post[0].reshape(1, D_MODEL), tm=out_tile, tiles_per_seq=seq // out_tile)
    return out.reshape(b, seq, D_MODEL)
```
